```python
import math
import jax, jax.numpy as jnp
from jax import lax
import numpy as np

D_MODEL = 1024
BATCH = 2
SEQ = 8192
DEPTH = 4
DEC_BATCH = 32
DEC_SEQ = 4
PAST_LEN = 8192
PAGE_SIZE = 128

N_MIXERS = 3
EPS = 1e-6
H_A = 16
HD_A = D_MODEL // H_A
E_A = H_A * HD_A
FOX_QBLOCK = 128
FOX_FORGET_BIAS = 4.0
H_B = 16
HD_B = D_MODEL // H_B
E_B = H_B * HD_B
DIL_GROUPS = ((128, 1), (512, 4), (2048, 16))
N_DIL = len(DIL_GROUPS)
H_C = 4
E_C = 2 * D_MODEL
DH_C = E_C // H_C
CONV_W = 4
MLSTM_CHUNK = 64

N_LAYERS_A = len(range(0, DEPTH, N_MIXERS))
N_LAYERS_B = len(range(1, DEPTH, N_MIXERS))
N_LAYERS_C = len(range(2, DEPTH, N_MIXERS))

kernel_name = "hybrid_fox_dilated_mlstm_step"


def rmsnorm(x, w):
    x32 = x.astype(jnp.float32)
    y = x32 * lax.rsqrt(jnp.mean(x32 * x32, axis=-1, keepdims=True) + EPS)
    return (y * w.astype(jnp.float32)).astype(x.dtype)


def alibi_slopes(n_heads):
    return (2.0 ** (-8.0 * (jnp.arange(n_heads, dtype=jnp.float32) + 1.0) / n_heads))


def fox_attention(q, k, v, logf):
    B, Tq, H, hd = q.shape
    Tk = k.shape[1]
    c = jnp.cumsum(logf.astype(jnp.float32), axis=1)
    cT = c.transpose(0, 2, 1)
    qb = math.gcd(Tq, FOX_QBLOCK)
    nb = Tq // qb
    q_blk = q.reshape(B, nb, qb, H, hd).transpose(1, 0, 2, 3, 4)
    c_blk = c[:, Tk - Tq:].reshape(B, nb, qb, H).transpose(1, 0, 3, 2)
    starts = Tk - Tq + jnp.arange(nb) * qb
    k_pos = jnp.arange(Tk)
    scale = hd ** -0.5

    def block(args):
        qi, ci, s0 = args
        s = jnp.einsum('bqhd,bkhd->bhqk', qi, k).astype(jnp.float32) * scale
        s = s + ci[..., None] - cT[:, :, None, :]
        q_pos = s0 + jnp.arange(qb)
        s = jnp.where(k_pos[None, :] <= q_pos[:, None], s, -jnp.inf)
        p = jax.nn.softmax(s, axis=-1)
        return jnp.einsum('bhqk,bkhd->bqhd', p.astype(v.dtype), v)

    o = lax.map(block, (q_blk, c_blk, starts))
    return o.transpose(1, 0, 2, 3, 4).reshape(B, Tq, H, hd)


def fox_mixer(h, w_in, b_f, w_out, k_past, v_past, logf_past):
    B, T, _ = h.shape
    proj = h @ w_in
    q, k, v, fpre, z = jnp.split(proj, [E_A, 2 * E_A, 3 * E_A, 3 * E_A + H_A], axis=-1)
    q = q.reshape(B, T, H_A, HD_A)
    k = k.reshape(B, T, H_A, HD_A)
    v = v.reshape(B, T, H_A, HD_A)
    logf = jax.nn.log_sigmoid(fpre.astype(jnp.float32) + b_f.astype(jnp.float32))
    k_all = jnp.concatenate([k_past.astype(k.dtype), k], axis=1)
    v_all = jnp.concatenate([v_past.astype(v.dtype), v], axis=1)
    lf_all = jnp.concatenate([logf_past.astype(jnp.float32), logf], axis=1)
    o = fox_attention(q, k_all, v_all, lf_all)
    y = (o.reshape(B, T, E_A) * jax.nn.silu(z)) @ w_out
    return y, k, v, logf


def dilated_band(q, k, v, window, dil, slopes):
    B, T, H, hd = q.shape
    R = window // dil
    L = T // dil
    nb = -(-L // R)
    Lp = nb * R

    def to_sub(x):
        x = x.reshape(B, L, dil, H, hd).transpose(0, 2, 1, 3, 4)
        x = jnp.pad(x, ((0, 0), (0, 0), (0, Lp - L), (0, 0), (0, 0)))
        return x.reshape(B, dil, nb, R, H, hd)

    def with_prev(x):
        prev = jnp.pad(x, ((0, 0), (0, 0), (1, 0), (0, 0), (0, 0), (0, 0)))[:, :, :-1]
        return jnp.concatenate([prev, x], axis=3)

    qs = to_sub(q)
    kb = with_prev(to_sub(k))
    vb = with_prev(to_sub(v))
    s = jnp.einsum('bgnqhd,bgnkhd->bgnhqk', qs, kb).astype(jnp.float32) * (hd ** -0.5)
    uq = jnp.arange(nb)[:, None] * R + jnp.arange(R)[None, :]
    uk = (jnp.arange(nb)[:, None] - 1) * R + jnp.arange(2 * R)[None, :]
    dist = uq[:, :, None] - uk[:, None, :]
    valid = (dist >= 0) & (dist <= R) & (uk[:, None, :] >= 0)
    s = s - slopes[:, None, None] * (dist * dil).astype(jnp.float32)[:, None]
    s = jnp.where(valid[:, None], s, -jnp.inf)
    lse = jax.nn.logsumexp(s, axis=-1)
    p = jnp.exp(s - lse[..., None])
    o = jnp.einsum('bgnhqk,bgnkhd->bgnqhd', p.astype(v.dtype), vb)
    o = o.reshape(B, dil, Lp, H, hd)[:, :, :L].transpose(0, 2, 1, 3, 4).reshape(B, T, H, hd)
    lse = lse.transpose(0, 1, 2, 4, 3).reshape(B, dil, Lp, H)[:, :, :L]
    lse = lse.transpose(0, 2, 1, 3).reshape(B, T, H)
    return o, lse


def dilated_gather(q, k_seq, v_seq, window, dil, slopes):
    B, S, H, hd = q.shape
    Ls = k_seq.shape[1]
    R = window // dil
    steps = jnp.arange(R + 1)
    idx = (Ls - S + jnp.arange(S))[:, None] - steps[None, :] * dil
    valid = idx >= 0
    idx_c = jnp.maximum(idx, 0)
    kg = k_seq[:, idx_c]
    vg = v_seq[:, idx_c]
    s = jnp.einsum('bqhd,bqkhd->bhqk', q, kg).astype(jnp.float32) * (hd ** -0.5)
    s = s - slopes[:, None, None] * (steps * dil).astype(jnp.float32)
    s = jnp.where(valid, s, -jnp.inf)
    lse = jax.nn.logsumexp(s, axis=-1)
    p = jnp.exp(s - lse[..., None])
    o = jnp.einsum('bhqk,bqkhd->bqhd', p.astype(vg.dtype), vg)
    return o, lse.transpose(0, 2, 1)


def dilated_mixer(h, w_in, w_out, bufs):
    B, T, _ = h.shape
    parts = jnp.split(h @ w_in, 3 * N_DIL + 1, axis=-1)
    z = parts[-1]
    slopes = alibi_slopes(H_B)
    outs, lses, new_bufs = [], [], []
    for g, (window, dil) in enumerate(DIL_GROUPS):
        q, k, v = (p.reshape(B, T, H_B, HD_B) for p in parts[3 * g:3 * g + 3])
        kv = jnp.stack([k, v], axis=2)
        if bufs is None:
            o, lse = dilated_band(q, k, v, window, dil, slopes)
            new_bufs.append(kv[:, T - min(window, T):])
        else:
            kv_seq = jnp.concatenate([bufs[g].astype(kv.dtype), kv], axis=1)
            o, lse = dilated_gather(q, kv_seq[:, :, 0], kv_seq[:, :, 1], window, dil, slopes)
            new_bufs.append(kv_seq[:, T:])
        outs.append(o)
        lses.append(lse)
    alpha = jax.nn.softmax(jnp.stack(lses, axis=0), axis=0)
    o = jnp.einsum('gbth,gbthd->bthd', alpha, jnp.stack(outs, axis=0).astype(jnp.float32))
    y = (o.reshape(B, T, E_B).astype(h.dtype) * jax.nn.silu(z)) @ w_out
    return y, tuple(new_bufs)


def mlstm_chunkwise(q, k, v, logi, logf, C0, n0, m0):
    B, T, H, dh = q.shape
    Lc = math.gcd(T, MLSTM_CHUNK)
    nc = T // Lc
    f32 = jnp.float32

    def chunks(x):
        return x.astype(f32).reshape(B, nc, Lc, H, dh).transpose(1, 0, 3, 2, 4)

    def gchunks(x):
        return x.astype(f32).reshape(B, nc, Lc, H).transpose(1, 0, 3, 2)

    tril = jnp.tril(jnp.ones((Lc, Lc), dtype=bool))

    def step(carry, inp):
        C, n, m = carry
        qc, kc, vc, li, lf = inp
        b = jnp.cumsum(lf, axis=-1)
        D = b[..., :, None] - b[..., None, :] + li[..., None, :]
        D = jnp.where(tril, D, -jnp.inf)
        m_t = jnp.maximum(b + m[..., None], jnp.max(D, axis=-1))
        Dexp = jnp.where(tril, jnp.exp(D - m_t[..., None]), 0.0)
        inter = jnp.exp(b + m[..., None] - m_t)
        W = jnp.einsum('bhtd,bhsd->bhts', qc, kc) * Dexp
        num = jnp.einsum('bhts,bhsd->bhtd', W, vc) + inter[..., None] * jnp.einsum('bhtd,bhde->bhte', qc, C)
        den = jnp.sum(W, axis=-1) + inter * jnp.einsum('bhtd,bhd->bht', qc, n)
        h = num / jnp.maximum(jnp.abs(den), jnp.exp(-m_t))[..., None]
        m_new = m_t[..., -1]
        g = jnp.exp(b[..., -1:] - b + li - m_new[..., None])
        decay = jnp.exp(b[..., -1] + m - m_new)
        C_new = decay[..., None, None] * C + jnp.einsum('bhs,bhsd,bhse->bhde', g, kc, vc)
        n_new = decay[..., None] * n + jnp.einsum('bhs,bhsd->bhd', g, kc)
        return (C_new, n_new, m_new), h

    (C, n, m), hs = lax.scan(step, (C0.astype(f32), n0.astype(f32), m0.astype(f32)),
                             (chunks(q), chunks(k), chunks(v), gchunks(logi), gchunks(logf)))
    h = hs.transpose(1, 0, 3, 2, 4).reshape(B, T, H, dh)
    return h, C, n, m


def mlstm_mixer(h, w_in, b_i, b_f, conv_w, conv_b, norm_w, w_out, conv_state, C0, n0, m0):
    B, T, _ = h.shape
    proj = h @ w_in
    qk_pre, v, o_pre, z, ipre, fpre = jnp.split(
        proj, [2 * E_C, 3 * E_C, 4 * E_C, 5 * E_C, 5 * E_C + H_C], axis=-1)
    padded = jnp.concatenate([conv_state.astype(qk_pre.dtype), qk_pre], axis=1)
    conv = conv_b + sum(padded[:, w:w + T] * conv_w[w] for w in range(CONV_W))
    new_conv = padded[:, T:]
    q, k = jnp.split(jax.nn.silu(conv), 2, axis=-1)
    q = q.reshape(B, T, H_C, DH_C)
    k = k.reshape(B, T, H_C, DH_C) * (DH_C ** -0.5)
    v = v.reshape(B, T, H_C, DH_C)
    logi = ipre.astype(jnp.float32) + b_i.astype(jnp.float32)
    logf = jax.nn.log_sigmoid(fpre.astype(jnp.float32) + b_f.astype(jnp.float32))
    hh, C, n, m = mlstm_chunkwise(q, k, v, logi, logf, C0, n0, m0)
    hh = hh * jax.nn.sigmoid(o_pre.astype(jnp.float32)).reshape(B, T, H_C, DH_C)
    mu = jnp.mean(hh, axis=-1, keepdims=True)
    var = jnp.mean(jnp.square(hh - mu), axis=-1, keepdims=True)
    hn = ((hh - mu) * lax.rsqrt(var + EPS)).reshape(B, T, E_C) * norm_w.astype(jnp.float32)
    y = (hn.astype(h.dtype) * jax.nn.silu(z)) @ w_out
    return y, C, n, m, new_conv


def setup_inputs(seed: int = 0) -> dict:
    key = jax.random.key(seed)
    ks = jax.random.split(key, 32)
    f32 = jnp.float32
    n_pages = PAST_LEN // PAGE_SIZE
    n_used = DEC_BATCH * n_pages
    n_pool = (5 * n_used + 3) // 4

    def nrm(k, shape, scale=1.0):
        return jax.random.normal(k, shape, f32) * scale

    inp = {}
    inp["x_prompt"] = nrm(ks[0], (BATCH, SEQ, D_MODEL))
    inp["x_sample"] = nrm(ks[1], (DEC_BATCH, DEC_SEQ, D_MODEL))
    inp["cache_fox_k"] = nrm(ks[2], (N_LAYERS_A, n_pool, PAGE_SIZE, H_A, HD_A))
    inp["cache_fox_v"] = nrm(ks[3], (N_LAYERS_A, n_pool, PAGE_SIZE, H_A, HD_A))
    inp["cache_fox_logf"] = jax.nn.log_sigmoid(
        FOX_FORGET_BIAS + nrm(ks[4], (N_LAYERS_A, n_pool, PAGE_SIZE, H_A)))
    for g, (window, dil) in enumerate(DIL_GROUPS):
        inp["cache_dil%d_kv" % g] = nrm(
            ks[5 + g], (N_LAYERS_B, DEC_BATCH, min(window, PAST_LEN), 2, H_B, HD_B))
    inp["state_mlstm_C"] = nrm(ks[8], (N_LAYERS_C, DEC_BATCH, H_C, DH_C, DH_C), 0.02)
    inp["state_mlstm_n"] = nrm(ks[9], (N_LAYERS_C, DEC_BATCH, H_C, DH_C), 0.1)
    inp["state_mlstm_m"] = nrm(ks[10], (N_LAYERS_C, DEC_BATCH, H_C))
    inp["state_mlstm_conv"] = nrm(ks[11], (N_LAYERS_C, DEC_BATCH, CONV_W - 1, 2 * E_C))
    inp["page_table"] = jax.random.permutation(ks[12], n_pool)[:n_used].reshape(
        DEC_BATCH, n_pages).astype(jnp.int32)
    inp["norm_w"] = 1.0 + nrm(ks[13], (DEPTH, D_MODEL), 0.05)
    inp["final_norm_w"] = 1.0 + nrm(ks[14], (D_MODEL,), 0.05)
    inp["fox_w_in"] = nrm(ks[15], (N_LAYERS_A, D_MODEL, 4 * E_A + H_A), D_MODEL ** -0.5)
    inp["fox_b_f"] = FOX_FORGET_BIAS + nrm(ks[16], (N_LAYERS_A, H_A), 0.5)
    inp["fox_w_out"] = nrm(ks[17], (N_LAYERS_A, E_A, D_MODEL), E_A ** -0.5)
    inp["dil_w_in"] = nrm(ks[18], (N_LAYERS_B, D_MODEL, (3 * N_DIL + 1) * E_B), D_MODEL ** -0.5)
    inp["dil_w_out"] = nrm(ks[19], (N_LAYERS_B, E_B, D_MODEL), E_B ** -0.5)
    inp["mlstm_w_in"] = nrm(ks[20], (N_LAYERS_C, D_MODEL, 5 * E_C + 2 * H_C), D_MODEL ** -0.5)
    inp["mlstm_b_i"] = nrm(ks[21], (N_LAYERS_C, H_C), 0.1)
    inp["mlstm_b_f"] = jnp.linspace(3.0, 6.0, H_C, dtype=f32)[None, :] + nrm(ks[22], (N_LAYERS_C, H_C), 0.1)
    inp["mlstm_conv_w"] = nrm(ks[23], (N_LAYERS_C, CONV_W, 2 * E_C), CONV_W ** -0.5)
    inp["mlstm_conv_b"] = nrm(ks[24], (N_LAYERS_C, 2 * E_C), 0.02)
    inp["mlstm_norm_w"] = 1.0 + nrm(ks[25], (N_LAYERS_C, E_C), 0.05)
    inp["mlstm_w_out"] = nrm(ks[26], (N_LAYERS_C, E_C, D_MODEL), E_C ** -0.5)
    return inp


def reference(x_prompt, x_sample, cache_fox_k, cache_fox_v, cache_fox_logf,
              cache_dil0_kv, cache_dil1_kv, cache_dil2_kv,
              state_mlstm_C, state_mlstm_n, state_mlstm_m, state_mlstm_conv, page_table,
              norm_w, final_norm_w, fox_w_in, fox_b_f, fox_w_out, dil_w_in, dil_w_out,
              mlstm_w_in, mlstm_b_i, mlstm_b_f, mlstm_conv_w, mlstm_conv_b, mlstm_norm_w,
              mlstm_w_out):
    f32 = jnp.float32
    bp = x_prompt.shape[0]
    bs = x_sample.shape[0]
    past = page_table.shape[1] * PAGE_SIZE
    dil_caches = (cache_dil0_kv, cache_dil1_kv, cache_dil2_kv)
    xp, xs = x_prompt, x_sample
    fox_p, fox_s, dil_p, dil_s, ml_p, ml_s = [], [], [], [], [], []
    for i in range(DEPTH):
        j = i // N_MIXERS
        kind = i % N_MIXERS
        hp = rmsnorm(xp, norm_w[i])
        hs = rmsnorm(xs, norm_w[i])
        if kind == 0:
            no_kv = jnp.zeros((bp, 0, H_A, HD_A), hp.dtype)
            no_lf = jnp.zeros((bp, 0, H_A), f32)
            yp, *new_p = fox_mixer(hp, fox_w_in[j], fox_b_f[j], fox_w_out[j], no_kv, no_kv, no_lf)
            k_past = cache_fox_k[j][page_table].reshape(bs, past, H_A, HD_A)
            v_past = cache_fox_v[j][page_table].reshape(bs, past, H_A, HD_A)
            lf_past = cache_fox_logf[j][page_table].reshape(bs, past, H_A)
            ys, *new_s = fox_mixer(hs, fox_w_in[j], fox_b_f[j], fox_w_out[j], k_past, v_past, lf_past)
            fox_p.append(new_p)
            fox_s.append(new_s)
        elif kind == 1:
            yp, new_p = dilated_mixer(hp, dil_w_in[j], dil_w_out[j], None)
            ys, new_s = dilated_mixer(hs, dil_w_in[j], dil_w_out[j],
                                      (dil_caches[0][j], dil_caches[1][j], dil_caches[2][j]))
            dil_p.append(new_p)
            dil_s.append(new_s)
        else:
            conv0 = jnp.zeros((bp, CONV_W - 1, 2 * E_C), hp.dtype)
            C0 = jnp.zeros((bp, H_C, DH_C, DH_C), f32)
            n0 = jnp.zeros((bp, H_C, DH_C), f32)
            m0 = jnp.zeros((bp, H_C), f32)
            yp, *new_p = mlstm_mixer(hp, mlstm_w_in[j], mlstm_b_i[j], mlstm_b_f[j], mlstm_conv_w[j],
                                     mlstm_conv_b[j], mlstm_norm_w[j], mlstm_w_out[j], conv0, C0, n0, m0)
            ys, *new_s = mlstm_mixer(hs, mlstm_w_in[j], mlstm_b_i[j], mlstm_b_f[j], mlstm_conv_w[j],
                                     mlstm_conv_b[j], mlstm_norm_w[j], mlstm_w_out[j],
                                     state_mlstm_conv[j], state_mlstm_C[j], state_mlstm_n[j],
                                     state_mlstm_m[j])
            ml_p.append(new_p)
            ml_s.append(new_s)
        xp = xp + yp.astype(xp.dtype)
        xs = xs + ys.astype(xs.dtype)

    def stk(lst, f):
        return jnp.stack([e[f] for e in lst], axis=0)

    y_prompt = rmsnorm(xp, final_norm_w)
    y_sample = rmsnorm(xs, final_norm_w)
    return (y_prompt, y_sample,
            stk(fox_p, 0), stk(fox_p, 1), stk(fox_p, 2),
            stk(fox_s, 0), stk(fox_s, 1), stk(fox_s, 2),
            stk(dil_p, 0), stk(dil_p, 1), stk(dil_p, 2),
            stk(dil_s, 0), stk(dil_s, 1), stk(dil_s, 2),
            stk(ml_p, 0), stk(ml_p, 1), stk(ml_p, 2), stk(ml_p, 3),
            stk(ml_s, 0), stk(ml_s, 1), stk(ml_s, 2), stk(ml_s, 3))
```

```python
import functools
import math

import numpy as np
import jax
import jax.numpy as jnp
from jax import lax
from jax.experimental import pallas as pl
from jax.experimental.pallas import tpu as pltpu

F32 = jnp.float32
BF16 = jnp.bfloat16
NEG = -1e30
EPS = 1e-6

D_MODEL = 1024
PAGE = 128
N_HEAD_A = 16
HD_A = 64
N_HEAD_B = 16
HD_B = 64
DIL_GROUPS = ((128, 1), (512, 4), (2048, 16))
BAND = 128
N_HEAD_C = 4
E_C = 2 * D_MODEL
DH_C = E_C // N_HEAD_C
CONV_W = 4
FOX_FORGET_LANES = 128
VMEM_LIMIT = 56 * 1024 * 1024


def _cparams(sem):
    return pltpu.CompilerParams(dimension_semantics=sem, vmem_limit_bytes=VMEM_LIMIT)


def _split3(x):
    hi = x.astype(BF16)
    r = x - hi.astype(F32)
    mid = r.astype(BF16)
    lo = (r - mid.astype(F32)).astype(BF16)
    return hi, mid, lo


def _dot(a, b):
    return jnp.dot(a, b, preferred_element_type=F32)


def _dot_nt(a, b):
    return lax.dot_general(a, b, (((1,), (1,)), ((), ())), preferred_element_type=F32)


def _dot_tn(a, b):
    return lax.dot_general(a, b, (((0,), (0,)), ((), ())), preferred_element_type=F32)


def _dot3_left(x, sel):
    hi, mid, lo = _split3(x)
    return _dot(hi, sel) + _dot(mid, sel) + _dot(lo, sel)


def _dot3_right(sel, x):
    hi, mid, lo = _split3(x)
    return _dot(sel, hi) + _dot(sel, mid) + _dot(sel, lo)


def _log_sigmoid(x):
    return jnp.minimum(x, 0.0) - jnp.log1p(jnp.exp(-jnp.abs(x)))


def _silu(z):
    return z / (1.0 + jnp.exp(-z))


def _sigmoid(z):
    return 1.0 / (1.0 + jnp.exp(-z))


def _norm_proj_kernel(x_ref, nw_ref, w_ref, wgh_ref, wgl_ref, o_ref, g_ref, h_ref):
    @pl.when(pl.program_id(1) == 0)
    def _():
        x = x_ref[...]
        h = x * lax.rsqrt(jnp.mean(x * x, axis=-1, keepdims=True) + EPS) * nw_ref[...]
        hb = h.astype(BF16)
        hl = (h - hb.astype(F32)).astype(BF16)
        h_ref[...] = hb
        g_ref[...] = _dot(hb, wgh_ref[...]) + _dot(hb, wgl_ref[...]) + _dot(hl, wgh_ref[...])

    o_ref[...] = _dot(h_ref[...], w_ref[...])


def norm_proj(x, nw, w, wg, tm, tn):
    n, d = x.shape
    nout = w.shape[1]
    wgp = jnp.zeros((d, 128), F32).at[:, : wg.shape[1]].set(wg)
    wgh = wgp.astype(BF16)
    wgl = (wgp - wgh.astype(F32)).astype(BF16)
    return pl.pallas_call(
        _norm_proj_kernel,
        grid=(n // tm, nout // tn),
        in_specs=[
            pl.BlockSpec((tm, d), lambda i, j: (i, 0)),
            pl.BlockSpec((1, d), lambda i, j: (0, 0)),
            pl.BlockSpec((d, tn), lambda i, j: (0, j)),
            pl.BlockSpec((d, 128), lambda i, j: (0, 0)),
            pl.BlockSpec((d, 128), lambda i, j: (0, 0)),
        ],
        out_specs=[
            pl.BlockSpec((tm, tn), lambda i, j: (i, j)),
            pl.BlockSpec((tm, 128), lambda i, j: (i, 0)),
        ],
        out_shape=[jax.ShapeDtypeStruct((n, nout), F32), jax.ShapeDtypeStruct((n, 128), F32)],
        scratch_shapes=[pltpu.VMEM((tm, d), BF16)],
        compiler_params=_cparams(("parallel", "arbitrary")),
        name="norm_proj",
    )(x, nw.reshape(1, d), w.astype(BF16), wgh, wgl)


def _out_proj_kernel(a_ref, z_ref, w_ref, x_ref, fw_ref, o_ref, *, final_norm):
    u = (a_ref[...] * _silu(z_ref[...])).astype(BF16)
    y = x_ref[...] + _dot(u, w_ref[...])
    if final_norm:
        y = y * lax.rsqrt(jnp.mean(y * y, axis=-1, keepdims=True) + EPS) * fw_ref[...]
    o_ref[...] = y


def out_proj(a, a_col, z, z_col, w, x, fw, tm, final_norm):
    n, d = x.shape
    e = w.shape[0]
    return pl.pallas_call(
        functools.partial(_out_proj_kernel, final_norm=final_norm),
        grid=(n // tm,),
        in_specs=[
            pl.BlockSpec((tm, e), lambda i: (i, a_col)),
            pl.BlockSpec((tm, e), lambda i: (i, z_col)),
            pl.BlockSpec((e, d), lambda i: (0, 0)),
            pl.BlockSpec((tm, d), lambda i: (i, 0)),
            pl.BlockSpec((1, d), lambda i: (0, 0)),
        ],
        out_specs=pl.BlockSpec((tm, d), lambda i: (i, 0)),
        out_shape=jax.ShapeDtypeStruct((n, d), F32),
        compiler_params=_cparams(("parallel",)),
        name="out_proj",
    )(a, z, w.astype(BF16), x, fw.reshape(1, d))


def _dil_out_proj_kernel(o0_ref, o1_ref, o2_ref, l0_ref, l1_ref, l2_ref, z_ref, w_ref, x_ref, o_ref):
    l0, l1, l2 = l0_ref[...], l1_ref[...], l2_ref[...]
    m = jnp.maximum(jnp.maximum(l0, l1), l2)
    e0, e1, e2 = jnp.exp(l0 - m), jnp.exp(l1 - m), jnp.exp(l2 - m)
    o = (e0 * o0_ref[...] + e1 * o1_ref[...] + e2 * o2_ref[...]) / (e0 + e1 + e2)
    u = (o * _silu(z_ref[...])).astype(BF16)
    o_ref[...] = x_ref[...] + _dot(u, w_ref[...])


def dil_out_proj(os_, ls_, z, z_col, w, x, tm):
    n, d = x.shape
    e = w.shape[0]
    blk = pl.BlockSpec((tm, e), lambda i: (i, 0))
    return pl.pallas_call(
        _dil_out_proj_kernel,
        grid=(n // tm,),
        in_specs=[blk] * 6 + [
            pl.BlockSpec((tm, e), lambda i: (i, z_col)),
            pl.BlockSpec((e, d), lambda i: (0, 0)),
            pl.BlockSpec((tm, d), lambda i: (i, 0)),
        ],
        out_specs=pl.BlockSpec((tm, d), lambda i: (i, 0)),
        out_shape=jax.ShapeDtypeStruct((n, d), F32),
        compiler_params=_cparams(("parallel",)),
        name="dil_out_proj",
    )(*os_, *ls_, z, w.astype(BF16), x)


AUG_Q_C = (0, 1, 2)
AUG_K_C = (3, 4, 5)
AUG_PER_HEAD = 6


def _fox_sel_matrices():
    selq = np.zeros((3, 128, D_MODEL), np.float32)
    selk = np.zeros((3, 128, D_MODEL), np.float32)
    oneq = np.zeros((1, D_MODEL), np.float32)
    onek = np.zeros((1, D_MODEL), np.float32)
    for hp in range(N_HEAD_A // 2):
        for hh in range(2):
            base = hp * 128 + hh * AUG_PER_HEAD
            for p in range(3):
                selq[p, 2 * hp + hh, base + AUG_Q_C[p]] = 1.0
                selk[p, 2 * hp + hh, base + AUG_K_C[p]] = 1.0
                oneq[0, base + AUG_K_C[p]] = 1.0
                onek[0, base + AUG_Q_C[p]] = 1.0
    return selq, selk, oneq, onek


def _fox_prep_kernel(q_ref, k_ref, v_ref, g_ref, bf_ref, selq_ref, selk_ref, oneq_ref, onek_ref,
                     qb_ref, kb_ref, vb_ref, qa_ref, ka_ref, lf_ref, carry_ref, *, tb):
    @pl.when(pl.program_id(1) == 0)
    def _():
        carry_ref[...] = jnp.zeros_like(carry_ref)

    lf = _log_sigmoid(g_ref[...] + bf_ref[...])
    lf_ref[...] = lf
    row = lax.broadcasted_iota(jnp.int32, (tb, tb), 0)
    col = lax.broadcasted_iota(jnp.int32, (tb, tb), 1)
    tril = (col <= row).astype(BF16)
    c = _dot3_right(tril, lf) + carry_ref[...]
    carry_ref[...] = c[tb - 1:tb, :]
    parts = _split3(c)
    qa = oneq_ref[...]
    ka = onek_ref[...]
    for p in range(3):
        qa = qa + _dot(parts[p], selq_ref[p])
        ka = ka - _dot(parts[p], selk_ref[p])
    qa_ref[...] = qa.astype(BF16)
    ka_ref[...] = ka.astype(BF16)
    qb_ref[...] = (q_ref[...] * (HD_A ** -0.5)).astype(BF16)
    kb_ref[...] = k_ref[...].astype(BF16)
    vb_ref[...] = v_ref[...].astype(BF16)


def fox_prep(proj, gates, b_f, bsz, t, tb):
    n = bsz * t
    nb = t // tb
    selq, selk, oneq, onek = _fox_sel_matrices()
    bfp = jnp.zeros((1, 128), F32).at[0, :N_HEAD_A].set(b_f)
    row_blk = lambda c: pl.BlockSpec((tb, D_MODEL), lambda b, i, c=c: (b * nb + i, c))
    const3 = pl.BlockSpec((3, 128, D_MODEL), lambda b, i: (0, 0, 0))
    const1 = pl.BlockSpec((1, D_MODEL), lambda b, i: (0, 0))
    out_blk = pl.BlockSpec((tb, D_MODEL), lambda b, i: (b * nb + i, 0))
    g_blk = pl.BlockSpec((tb, 128), lambda b, i: (b * nb + i, 0))
    return pl.pallas_call(
        functools.partial(_fox_prep_kernel, tb=tb),
        grid=(bsz, nb),
        in_specs=[row_blk(0), row_blk(1), row_blk(2), g_blk,
                  pl.BlockSpec((1, 128), lambda b, i: (0, 0)), const3, const3, const1, const1],
        out_specs=[out_blk] * 5 + [g_blk],
        out_shape=[jax.ShapeDtypeStruct((n, D_MODEL), BF16)] * 5 + [jax.ShapeDtypeStruct((n, 128), F32)],
        scratch_shapes=[pltpu.VMEM((1, 128), F32)],
        compiler_params=_cparams(("parallel", "arbitrary")),
        name="fox_prep",
    )(proj, proj, proj, gates, bfp, jnp.asarray(selq, BF16), jnp.asarray(selk, BF16),
      jnp.asarray(oneq), jnp.asarray(onek))


def _fox_attn_kernel(qi_ref, kj_ref, qb_ref, qa_ref, kb_ref, ka_ref, vb_ref, o_ref,
                     lhs_ref, m_ref, l_ref, acc_ref, *, tq):
    s_idx = pl.program_id(2)
    qi = qi_ref[s_idx]
    kj = kj_ref[s_idx]

    @pl.when(kj == 0)
    def _init():
        qcat = jnp.concatenate([qb_ref[...], qa_ref[...]], axis=1)
        lane = lax.broadcasted_iota(jnp.int32, (1, 256), 1)
        for hh in range(2):
            lo = 128 + hh * AUG_PER_HEAD
            keep = ((lane >= hh * 64) & (lane < hh * 64 + 64)) | ((lane >= lo) & (lane < lo + AUG_PER_HEAD))
            lhs_ref[hh] = jnp.where(keep, qcat, jnp.zeros_like(qcat))
        m_ref[...] = jnp.full_like(m_ref, NEG)
        l_ref[...] = jnp.zeros_like(l_ref)
        acc_ref[...] = jnp.zeros_like(acc_ref)

    def step(diagonal):
        kcat = jnp.concatenate([kb_ref[...], ka_ref[...]], axis=1)
        v = vb_ref[...]
        for hh in range(2):
            s = _dot_nt(lhs_ref[hh], kcat)
            if diagonal:
                row = lax.broadcasted_iota(jnp.int32, s.shape, 0)
                col = lax.broadcasted_iota(jnp.int32, s.shape, 1)
                s = jnp.where(col <= row, s, NEG)
            m_prev = m_ref[hh]
            m_new = jnp.maximum(m_prev, jnp.max(s, axis=1, keepdims=True))
            alpha = jnp.exp(m_prev - m_new)
            p = jnp.exp(s - m_new)
            l_ref[hh] = alpha * l_ref[hh] + jnp.sum(p, axis=1, keepdims=True)
            acc_ref[hh] = alpha * acc_ref[hh] + _dot(p.astype(BF16), v)
            m_ref[hh] = m_new

    @pl.when(kj < qi)
    def _off():
        step(False)

    @pl.when(kj == qi)
    def _diag():
        step(True)
        lane = lax.broadcasted_iota(jnp.int32, (tq, 128), 1)
        o_ref[...] = jnp.where(lane < 64, acc_ref[0] / l_ref[0], acc_ref[1] / l_ref[1])


def fox_attention_prompt(qb, qa, kb, ka, vb, bsz, t, tq):
    n = bsz * t
    nq = t // tq
    qi = np.concatenate([np.full(i + 1, i, np.int32) for i in range(nq)])
    kj = np.concatenate([np.arange(i + 1, dtype=np.int32) for i in range(nq)])
    q_blk = pl.BlockSpec((tq, 128), lambda b, hp, s, qi_r, kj_r: (b * nq + qi_r[s], hp))
    k_blk = pl.BlockSpec((tq, 128), lambda b, hp, s, qi_r, kj_r: (b * nq + kj_r[s], hp))
    return pl.pallas_call(
        functools.partial(_fox_attn_kernel, tq=tq),
        grid_spec=pltpu.PrefetchScalarGridSpec(
            num_scalar_prefetch=2,
            grid=(bsz, N_HEAD_A // 2, len(qi)),
            in_specs=[q_blk, q_blk, k_blk, k_blk, k_blk],
            out_specs=q_blk,
            scratch_shapes=[pltpu.VMEM((2, tq, 256), BF16), pltpu.VMEM((2, tq, 1), F32),
                            pltpu.VMEM((2, tq, 1), F32), pltpu.VMEM((2, tq, 128), F32)],
        ),
        out_shape=jax.ShapeDtypeStruct((n, D_MODEL), F32),
        compiler_params=_cparams(("parallel", "parallel", "arbitrary")),
        name="fox_attn",
    )(jnp.asarray(qi), jnp.asarray(kj), qb, qa, kb, ka, vb)


N_NEW = 4
ROWS = N_NEW * N_HEAD_A


def _fox_sample_kernel(pt_ref, qt_ref, ck_ref, cv_ref, clf_ref, kn_ref, vn_ref, gn_ref, bf_ref,
                       e16_ref, e128_ref, hmask_ref, o_ref, lfn_ref,
                       m_ref, l_ref, acc_ref, carry_ref, erow_ref, *, n_pages):
    p = pl.program_id(1)
    row = lax.broadcasted_iota(jnp.int32, (PAGE, PAGE), 0)
    col = lax.broadcasted_iota(jnp.int32, (PAGE, PAGE), 1)
    lane_tok = col // N_HEAD_A

    def update(k_f32, v_f32, bias_t, valid_t):
        s_t = _dot(k_f32.astype(BF16), qt_ref[0]) + bias_t
        if valid_t is not None:
            s_t = jnp.where(valid_t, s_t, NEG)
        s = s_t.T[:ROWS]
        m_prev = m_ref[...]
        m_new = jnp.maximum(m_prev, jnp.max(s, axis=1, keepdims=True))
        alpha = jnp.exp(m_prev - m_new)
        pr = jnp.exp(s - m_new)
        l_ref[...] = alpha * l_ref[...] + jnp.sum(pr, axis=1, keepdims=True)
        acc_ref[...] = alpha * acc_ref[...] + _dot(pr.astype(BF16), v_f32.astype(BF16))
        m_ref[...] = m_new

    @pl.when(p == 0)
    def _new_tokens():
        m_ref[...] = jnp.full_like(m_ref, NEG)
        l_ref[...] = jnp.zeros_like(l_ref)
        acc_ref[...] = jnp.zeros_like(acc_ref)
        carry_ref[...] = jnp.zeros_like(carry_ref)
        lfn = _log_sigmoid(gn_ref[0] + bf_ref[...])
        lfn_ref[0] = lfn[:8]
        x = _dot3_left(lfn, e128_ref[...])
        ecum = jnp.zeros_like(x)
        for r in range(N_NEW):
            ecum = ecum + jnp.where(row >= r, x[r:r + 1, :], 0.0)
        erow = jnp.sum(jnp.where(row == lane_tok, ecum, 0.0), axis=0, keepdims=True)
        erow_ref[...] = erow
        valid = (row <= lane_tok) & (row < N_NEW)
        update(kn_ref[0], vn_ref[0], erow - ecum, valid)

    @pl.when(p > 0)
    def _page():
        x = _dot3_left(clf_ref[0], e16_ref[...])
        upper = (col > row).astype(BF16)
        d = _dot3_right(upper, x) + carry_ref[...]
        carry_ref[...] = carry_ref[...] + jnp.sum(x, axis=0, keepdims=True)
        update(ck_ref[0], cv_ref[0], erow_ref[...] + d, None)

    @pl.when(p == n_pages)
    def _finish():
        o = acc_ref[...] / l_ref[...]
        o = o.reshape(N_NEW, N_HEAD_A, D_MODEL) * hmask_ref[...][None]
        o_ref[0] = jnp.sum(o, axis=1)


def fox_attention_sample(q, knew, vnew, gates, b_f, cache_k, cache_v, cache_lf, page_table):
    bs, n_pages = page_table.shape
    n_pool = cache_k.shape[0]
    h_idx = np.arange(D_MODEL) // HD_A
    hmask = (h_idx[None, :] == np.arange(N_HEAD_A)[:, None]).astype(np.float32)
    qh = (q * (HD_A ** -0.5)).reshape(bs, N_NEW, N_HEAD_A, HD_A)
    qt = jnp.einsum("bihd,hg->bhdig", qh, jnp.eye(N_HEAD_A, dtype=F32)).reshape(bs, D_MODEL, ROWS)
    qt = jnp.pad(qt, ((0, 0), (0, 0), (0, 128 - ROWS))).astype(BF16)
    pad_rows = lambda a: jnp.pad(a, ((0, 0), (0, PAGE - a.shape[1]), (0, 0)))
    e16 = np.zeros((N_HEAD_A, 128), np.float32)
    for i in range(N_NEW):
        e16[np.arange(N_HEAD_A), i * N_HEAD_A + np.arange(N_HEAD_A)] = 1.0
    e128 = np.zeros((128, 128), np.float32)
    e128[:N_HEAD_A] = e16
    bfp = jnp.zeros((1, 128), F32).at[0, :N_HEAD_A].set(b_f)

    def page_map(b, p, pt):
        return (pt[b * n_pages + n_pages - 1 - jnp.maximum(p - 1, 0)], 0, 0)

    per_b3 = lambda shape: pl.BlockSpec(shape, lambda b, p, pt: (b, 0, 0))
    const2 = lambda shape: pl.BlockSpec(shape, lambda b, p, pt: (0, 0))
    return pl.pallas_call(
        functools.partial(_fox_sample_kernel, n_pages=n_pages),
        grid_spec=pltpu.PrefetchScalarGridSpec(
            num_scalar_prefetch=1,
            grid=(bs, n_pages + 1),
            in_specs=[
                per_b3((1, D_MODEL, 128)),
                pl.BlockSpec((1, PAGE, D_MODEL), page_map),
                pl.BlockSpec((1, PAGE, D_MODEL), page_map),
                pl.BlockSpec((1, PAGE, N_HEAD_A), page_map),
                per_b3((1, PAGE, D_MODEL)),
                per_b3((1, PAGE, D_MODEL)),
                per_b3((1, PAGE, 128)),
                const2((1, 128)),
                const2((N_HEAD_A, 128)),
                const2((128, 128)),
                const2((N_HEAD_A, D_MODEL)),
            ],
            out_specs=[per_b3((1, N_NEW, D_MODEL)), per_b3((1, 8, 128))],
            scratch_shapes=[pltpu.VMEM((ROWS, 1), F32), pltpu.VMEM((ROWS, 1), F32),
                            pltpu.VMEM((ROWS, D_MODEL), F32), pltpu.VMEM((1, 128), F32),
                            pltpu.VMEM((1, 128), F32)],
        ),
        out_shape=[jax.ShapeDtypeStruct((bs, N_NEW, D_MODEL), F32),
                   jax.ShapeDtypeStruct((bs, 8, 128), F32)],
        compiler_params=_cparams(("parallel", "arbitrary")),
        name="fox_sample",
    )(page_table.reshape(-1), qt,
      cache_k.reshape(n_pool, PAGE, D_MODEL), cache_v.reshape(n_pool, PAGE, D_MODEL), cache_lf,
      pad_rows(knew), pad_rows(vnew), pad_rows(gates), bfp,
      jnp.asarray(e16, BF16), jnp.asarray(e128, BF16), jnp.asarray(hmask))


def _alibi_slope(h):
    return float(2.0 ** (-8.0 * (h + 1.0) / N_HEAD_B))


def _dil_band_kernel(q_ref, kp_ref, kc_ref, vp_ref, vc_ref, o_ref, lse_ref, *, dil):
    n = pl.program_id(2)
    r = BAND
    qrow = lax.broadcasted_iota(jnp.int32, (r, 2 * r), 0)
    kcol = lax.broadcasted_iota(jnp.int32, (r, 2 * r), 1)
    dist = r + qrow - kcol
    valid = (dist >= 0) & (dist <= r) & ((kcol >= r) | (n > 0))
    distf = (dist * dil).astype(F32)
    for h in range(N_HEAD_B):
        hs = slice(h * HD_B, (h + 1) * HD_B)
        qh = (q_ref[:, hs] * (HD_B ** -0.5)).astype(BF16)
        kh = jnp.concatenate([kp_ref[:, hs], kc_ref[:, hs]], axis=0).astype(BF16)
        vh = jnp.concatenate([vp_ref[:, hs], vc_ref[:, hs]], axis=0).astype(BF16)
        s = _dot_nt(qh, kh) - _alibi_slope(h) * distf
        s = jnp.where(valid, s, NEG)
        m = jnp.max(s, axis=1, keepdims=True)
        p = jnp.exp(s - m)
        l = jnp.sum(p, axis=1, keepdims=True)
        o_ref[:, hs] = _dot(p.astype(BF16), vh) / l
        lse_ref[:, hs] = jnp.broadcast_to(m + jnp.log(l), (r, HD_B))


def dil_band_attention(proj, g, bsz, t):
    window, dil = DIL_GROUPS[g]
    assert window // dil == BAND
    length = t // dil
    nb = length // BAND
    ncol = proj.shape[1] // D_MODEL
    pv = proj.reshape(bsz * length, dil * proj.shape[1])

    def cur(c):
        return pl.BlockSpec((BAND, D_MODEL), lambda b, r, n, c=c: (b * nb + n, r * ncol + c))

    def prev(c):
        return pl.BlockSpec((BAND, D_MODEL), lambda b, r, n, c=c: (b * nb + jnp.maximum(n - 1, 0), r * ncol + c))

    out_blk = pl.BlockSpec((BAND, D_MODEL), lambda b, r, n: (b * nb + n, r))
    o, lse = pl.pallas_call(
        functools.partial(_dil_band_kernel, dil=dil),
        grid=(bsz, dil, nb),
        in_specs=[cur(3 * g), prev(3 * g + 1), cur(3 * g + 1), prev(3 * g + 2), cur(3 * g + 2)],
        out_specs=[out_blk, out_blk],
        out_shape=[jax.ShapeDtypeStruct((bsz * length, dil * D_MODEL), F32)] * 2,
        compiler_params=_cparams(("parallel", "parallel", "arbitrary")),
        name="dil_band%d" % g,
    )(pv, pv, pv, pv, pv)
    return o.reshape(bsz * t, D_MODEL), lse.reshape(bsz * t, D_MODEL)


def _dil_sample_kernel(q_ref, kn_ref, vn_ref, buf_ref, hmask_ref, slope_ref, o_ref, lse_ref, *, dil):
    hmask = hmask_ref[...]
    slope = slope_ref[...]
    m_idx = lax.broadcasted_iota(jnp.int32, (N_HEAD_B, BAND), 1)
    for i in range(N_NEW):
        ri = i if dil > 1 else 0
        kbuf = buf_ref[0, :, ri * 2 * D_MODEL: ri * 2 * D_MODEL + D_MODEL]
        vbuf = buf_ref[0, :, ri * 2 * D_MODEL + D_MODEL: (ri + 1) * 2 * D_MODEL]
        qbd = q_ref[0, i:i + 1, :] * (HD_B ** -0.5) * hmask
        s = _dot_nt(qbd.astype(BF16), kbuf.astype(BF16))
        if dil > 1:
            dist = (BAND - m_idx) * dil
            s = s - slope * dist.astype(F32)
            new_rows = (i,)
        else:
            dist = BAND + i - m_idx
            s = jnp.where(m_idx >= i, s - slope * dist.astype(F32), NEG)
            new_rows = tuple(range(i + 1))
        s_new = [jnp.sum(qbd * kn_ref[0, r:r + 1, :], axis=1, keepdims=True)
                 - slope * float((i - r) * dil) for r in new_rows]
        m = jnp.max(s, axis=1, keepdims=True)
        for sn in s_new:
            m = jnp.maximum(m, sn)
        p = jnp.exp(s - m)
        l = jnp.sum(p, axis=1, keepdims=True)
        o = _dot(p.astype(BF16), vbuf.astype(BF16))
        for r, sn in zip(new_rows, s_new):
            pn = jnp.exp(sn - m)
            l = l + pn
            o = o + pn * vn_ref[0, r:r + 1, :]
        o_ref[0, i:i + 1, :] = jnp.sum(o / l * hmask, axis=0, keepdims=True)
        lse_ref[0, i:i + 1, :] = jnp.sum((m + jnp.log(l)) * hmask, axis=0, keepdims=True)


def dil_sample_attention(proj3, g, buf):
    window, dil = DIL_GROUPS[g]
    bs = proj3.shape[0]
    assert buf.shape[1] == window and window == BAND * dil and N_NEW <= dil * BAND
    nres = 1 if dil == 1 else N_NEW
    bufv = buf.reshape(bs, BAND, dil * 2 * D_MODEL)
    h_idx = np.arange(D_MODEL) // HD_B
    hmask = (h_idx[None, :] == np.arange(N_HEAD_B)[:, None]).astype(np.float32)
    slope = np.array([[_alibi_slope(h)] for h in range(N_HEAD_B)], np.float32)
    col = lambda c: pl.BlockSpec((1, N_NEW, D_MODEL), lambda b, c=c: (b, 0, c))
    return pl.pallas_call(
        functools.partial(_dil_sample_kernel, dil=dil),
        grid=(bs,),
        in_specs=[col(3 * g), col(3 * g + 1), col(3 * g + 2),
                  pl.BlockSpec((1, BAND, nres * 2 * D_MODEL), lambda b: (b, 0, 0)),
                  pl.BlockSpec((N_HEAD_B, D_MODEL), lambda b: (0, 0)),
                  pl.BlockSpec((N_HEAD_B, 1), lambda b: (0, 0))],
        out_specs=[pl.BlockSpec((1, N_NEW, D_MODEL), lambda b: (b, 0, 0))] * 2,
        out_shape=[jax.ShapeDtypeStruct((bs, N_NEW, D_MODEL), F32)] * 2,
        compiler_params=_cparams(("parallel",)),
        name="dil_sample%d" % g,
    )(proj3, proj3, proj3, bufv, jnp.asarray(hmask), jnp.asarray(slope))


def _mlstm_kernel(qp_ref, kp_ref, v_ref, op_ref, g_ref, gb_ref, cwq_ref, cwk_ref, cbq_ref, cbk_ref,
                  nw_ref, csq_ref, csk_ref, c0_ref, n0_ref, m0_ref,
                  hn_ref, c_out_ref, n_out_ref, m_out_ref,
                  c_s, n_s, m_s, xq_s, xk_s, *, lc, n_valid, nc):
    h = pl.program_id(1)
    c = pl.program_id(2)

    @pl.when(c == 0)
    def _():
        c_s[...] = c0_ref[0, 0]
        n_s[...] = n0_ref[0, 0]
        m_s[...] = m0_ref[0, 0]
        xq_s[0:8, :] = csq_ref[0]
        xk_s[0:8, :] = csk_ref[0]

    def conv(x_ref, xs, cw_ref, cb_ref):
        xs[8:8 + lc, :] = x_ref[...]
        acc = cb_ref[...] + cw_ref[0:1, :] * xs[pl.ds(8 - (CONV_W - 1), lc), :]
        for w in range(1, CONV_W):
            acc = acc + cw_ref[w:w + 1, :] * xs[pl.ds(8 - (CONV_W - 1) + w, lc), :]
        xs[0:8, :] = xs[lc:lc + 8, :]
        return _silu(acc)

    q = conv(qp_ref, xq_s, cwq_ref, cbq_ref)
    k = conv(kp_ref, xk_s, cwk_ref, cbk_ref) * (DH_C ** -0.5)
    v = v_ref[...]

    g = g_ref[...] + gb_ref[...]
    lane = lax.broadcasted_iota(jnp.int32, g.shape, 1)
    li_col = jnp.sum(jnp.where(lane == h, g, 0.0), axis=1, keepdims=True)
    lf_col = _log_sigmoid(jnp.sum(jnp.where(lane == N_HEAD_C + h, g, 0.0), axis=1, keepdims=True))
    if n_valid < lc:
        rows = lax.broadcasted_iota(jnp.int32, (lc, 1), 0)
        li_col = jnp.where(rows < n_valid, li_col, NEG)
        lf_col = jnp.where(rows < n_valid, lf_col, 0.0)
    row = lax.broadcasted_iota(jnp.int32, (lc, lc), 0)
    col = lax.broadcasted_iota(jnp.int32, (lc, lc), 1)
    eye = row == col
    tril = col <= row
    to_row = lambda x_col: jnp.sum(jnp.where(eye, x_col, 0.0), axis=0, keepdims=True)
    lf_row = to_row(lf_col)
    li_row = to_row(li_col)
    b_col = jnp.sum(jnp.where(tril, lf_row, 0.0), axis=1, keepdims=True)
    b_row = to_row(b_col)
    m_prev = m_s[...]
    d = jnp.where(tril, b_col - b_row + li_row, NEG)
    m_t = jnp.maximum(b_col + m_prev, jnp.max(d, axis=1, keepdims=True))
    dexp = jnp.where(tril, jnp.exp(d - m_t), 0.0)
    inter = jnp.exp(b_col + m_prev - m_t)
    qb, kb, vb = q.astype(BF16), k.astype(BF16), v.astype(BF16)
    w = _dot_nt(qb, kb) * dexp
    num = _dot(w.astype(BF16), vb) + inter * _dot(qb, c_s[...].astype(BF16))
    den = jnp.sum(w, axis=1, keepdims=True) + inter * jnp.sum(q * n_s[...], axis=1, keepdims=True)
    hh = num / jnp.maximum(jnp.abs(den), jnp.exp(-m_t))
    m_new = m_t[lc - 1:lc, :]
    b_last = b_col[lc - 1:lc, :]
    g_col = jnp.exp(b_last - b_col + li_col - m_new)
    decay = jnp.exp(b_last + m_prev - m_new)
    kg = k * g_col
    c_s[...] = decay * c_s[...] + _dot_tn(kg.astype(BF16), vb)
    n_s[...] = decay * n_s[...] + jnp.sum(kg, axis=0, keepdims=True)
    m_s[...] = m_new

    ho = hh * _sigmoid(op_ref[...])
    mu = jnp.mean(ho, axis=1, keepdims=True)
    var = jnp.mean(jnp.square(ho - mu), axis=1, keepdims=True)
    hn_ref[...] = (ho - mu) * lax.rsqrt(var + EPS) * nw_ref[...]

    @pl.when(c == nc - 1)
    def _():
        c_out_ref[0, 0] = c_s[...]
        n_out_ref[0, 0] = n_s[...]
        m_out_ref[0, 0] = m_s[...]


def mlstm_mix(proj, gates, b_i, b_f, conv_w, conv_b, norm_w, conv_state, c0, n0, m0, bsz, t, lc, n_valid):
    n = bsz * t
    nc = t // lc
    assert n_valid == lc or nc == 1
    nh = N_HEAD_C
    gb = jnp.zeros((1, 128), F32).at[0, :nh].set(b_i).at[0, nh:2 * nh].set(b_f)
    cs = jnp.pad(conv_state, ((0, 0), (8 - (CONV_W - 1), 0), (0, 0)))
    blk = lambda c0_: pl.BlockSpec((lc, DH_C), lambda b, h, c, c0_=c0_: (b * nc + c, c0_ + h))
    per_h = lambda rows, c0_: pl.BlockSpec((rows, DH_C), lambda b, h, c, c0_=c0_: (0, c0_ + h))
    st4 = lambda shape: pl.BlockSpec(shape, lambda b, h, c: (b, h, 0, 0))
    hn, c_out, n_out, m_out = pl.pallas_call(
        functools.partial(_mlstm_kernel, lc=lc, n_valid=n_valid, nc=nc),
        grid=(bsz, nh, nc),
        in_specs=[
            blk(0), blk(nh), blk(2 * nh), blk(3 * nh),
            pl.BlockSpec((lc, 128), lambda b, h, c: (b * nc + c, 0)),
            pl.BlockSpec((1, 128), lambda b, h, c: (0, 0)),
            per_h(CONV_W, 0), per_h(CONV_W, nh), per_h(1, 0), per_h(1, nh),
            per_h(1, 0),
            pl.BlockSpec((1, 8, DH_C), lambda b, h, c: (b, 0, h)),
            pl.BlockSpec((1, 8, DH_C), lambda b, h, c: (b, 0, nh + h)),
            st4((1, 1, DH_C, DH_C)), st4((1, 1, 1, DH_C)), st4((1, 1, 1, 1)),
        ],
        out_specs=[
            pl.BlockSpec((lc, DH_C), lambda b, h, c: (b * nc + c, h)),
            st4((1, 1, DH_C, DH_C)), st4((1, 1, 1, DH_C)), st4((1, 1, 1, 1)),
        ],
        out_shape=[
            jax.ShapeDtypeStruct((n, E_C), F32),
            jax.ShapeDtypeStruct((bsz, nh, DH_C, DH_C), F32),
            jax.ShapeDtypeStruct((bsz, nh, 1, DH_C), F32),
            jax.ShapeDtypeStruct((bsz, nh, 1, 1), F32),
        ],
        scratch_shapes=[pltpu.VMEM((DH_C, DH_C), F32), pltpu.VMEM((1, DH_C), F32), pltpu.VMEM((1, 1), F32),
                        pltpu.VMEM((lc + 8, DH_C), F32), pltpu.VMEM((lc + 8, DH_C), F32)],
        compiler_params=_cparams(("parallel", "parallel", "arbitrary")),
        name="mlstm",
    )(proj, proj, proj, proj, gates, gb, conv_w, conv_w, conv_b.reshape(1, -1), conv_b.reshape(1, -1),
      norm_w.reshape(1, -1), cs, cs, c0, n0.reshape(bsz, nh, 1, DH_C), m0.reshape(bsz, nh, 1, 1))
    return hn, c_out, n_out.reshape(bsz, nh, DH_C), m_out.reshape(bsz, nh)


TM_PROMPT = 512
TN_PROJ = 512
FOX_TQ = 512
FOX_PREP_TB = 256
MLSTM_LC = 256
SAMPLE_LC = 16


def _fox_weights(w_in):
    e = N_HEAD_A * HD_A
    w_main = jnp.concatenate([w_in[:, :3 * e], w_in[:, 3 * e + N_HEAD_A:]], axis=1)
    return w_main, w_in[:, 3 * e:3 * e + N_HEAD_A]


def fox_layer(xp, xs, nw, w_in, b_f, w_out, cache_k, cache_v, cache_lf, page_table, bp, tp, bs, fw, final):
    w_main, w_gate = _fox_weights(w_in)
    proj_p, gates_p = norm_proj(xp, nw, w_main, w_gate, TM_PROMPT, TN_PROJ)
    proj_s, gates_s = norm_proj(xs, nw, w_main, w_gate, xs.shape[0], TN_PROJ)
    qb, kb, vb, qa, ka, lf_p = fox_prep(proj_p, gates_p, b_f, bp, tp, FOX_PREP_TB)
    o_p = fox_attention_prompt(qb, qa, kb, ka, vb, bp, tp, FOX_TQ)
    yp = out_proj(o_p, 0, proj_p, 3, w_out, xp, fw, TM_PROMPT, final)
    ps3 = proj_s.reshape(bs, N_NEW, -1)
    k_s, v_s = ps3[:, :, D_MODEL:2 * D_MODEL], ps3[:, :, 2 * D_MODEL:3 * D_MODEL]
    o_s, lf_s = fox_attention_sample(ps3[:, :, :D_MODEL], k_s, v_s, gates_s.reshape(bs, N_NEW, 128), b_f,
                                     cache_k, cache_v, cache_lf, page_table)
    ys = out_proj(o_s.reshape(bs * N_NEW, D_MODEL), 0, proj_s, 3, w_out, xs, fw, xs.shape[0], final)
    heads = lambda a, b_, t_: a.reshape(b_, t_, N_HEAD_A, HD_A)
    new_p = (heads(proj_p[:, D_MODEL:2 * D_MODEL], bp, tp), heads(proj_p[:, 2 * D_MODEL:3 * D_MODEL], bp, tp),
             lf_p[:, :N_HEAD_A].reshape(bp, tp, N_HEAD_A))
    new_s = (heads(k_s, bs, N_NEW), heads(v_s, bs, N_NEW), lf_s[:, :N_NEW, :N_HEAD_A])
    return yp, ys, new_p, new_s


def dil_layer(xp, xs, nw, w_in, w_out, caches, bp, tp, bs):
    no_gate = jnp.zeros((D_MODEL, 1), F32)
    proj_p, _ = norm_proj(xp, nw, w_in, no_gate, TM_PROMPT, TN_PROJ)
    proj_s, _ = norm_proj(xs, nw, w_in, no_gate, xs.shape[0], TN_PROJ)
    z_col = 3 * len(DIL_GROUPS)
    res_p = [dil_band_attention(proj_p, g, bp, tp) for g in range(len(DIL_GROUPS))]
    yp = dil_out_proj([r[0] for r in res_p], [r[1] for r in res_p], proj_p, z_col, w_out, xp, TM_PROMPT)
    ps3 = proj_s.reshape(bs, N_NEW, -1)
    res_s = [dil_sample_attention(ps3, g, caches[g]) for g in range(len(DIL_GROUPS))]
    flat = lambda a: a.reshape(bs * N_NEW, D_MODEL)
    ys = dil_out_proj([flat(r[0]) for r in res_s], [flat(r[1]) for r in res_s], proj_s, z_col, w_out, xs,
                      xs.shape[0])
    new_p, new_s = [], []
    for g, (window, _) in enumerate(DIL_GROUPS):
        def kv(pr, b_, t_, keep):
            rows = pr.reshape(b_, t_, -1)[:, t_ - keep:, (3 * g + 1) * D_MODEL:(3 * g + 3) * D_MODEL]
            return rows.reshape(b_, keep, 2, N_HEAD_B, HD_B)
        new_p.append(kv(proj_p, bp, tp, min(window, tp)))
        new_s.append(jnp.concatenate([caches[g][:, N_NEW:], kv(proj_s, bs, N_NEW, N_NEW)], axis=1))
    return yp, ys, tuple(new_p), tuple(new_s)


def mlstm_layer(xp, xs, nw, w_in, b_i, b_f, conv_w, conv_b, norm_w, w_out, conv_state, c0, n0, m0, bp, tp, bs, fw):
    n_main = 5 * E_C
    w_main, w_gate = w_in[:, :n_main], w_in[:, n_main:]
    proj_p, gates_p = norm_proj(xp, nw, w_main, w_gate, TM_PROMPT, TN_PROJ)
    proj_s, gates_s = norm_proj(xs, nw, w_main, w_gate, xs.shape[0], TN_PROJ)
    zeros = lambda *shape: jnp.zeros(shape, F32)
    hn_p, c_p, n_p, m_p = mlstm_mix(proj_p, gates_p, b_i, b_f, conv_w, conv_b, norm_w,
                                    zeros(bp, CONV_W - 1, 2 * E_C), zeros(bp, N_HEAD_C, DH_C, DH_C),
                                    zeros(bp, N_HEAD_C, DH_C), zeros(bp, N_HEAD_C), bp, tp, MLSTM_LC, MLSTM_LC)
    yp = out_proj(hn_p, 0, proj_p, 4, w_out, xp, fw, TM_PROMPT, False)
    pad_t = lambda a: jnp.pad(a.reshape(bs, N_NEW, -1), ((0, 0), (0, SAMPLE_LC - N_NEW), (0, 0))).reshape(
        bs * SAMPLE_LC, -1)
    hn_s, c_s, n_s, m_s = mlstm_mix(pad_t(proj_s), pad_t(gates_s), b_i, b_f, conv_w, conv_b, norm_w,
                                    conv_state, c0, n0, m0, bs, SAMPLE_LC, SAMPLE_LC, N_NEW)
    hn_s = hn_s.reshape(bs, SAMPLE_LC, E_C)[:, :N_NEW].reshape(bs * N_NEW, E_C)
    ys = out_proj(hn_s, 0, proj_s, 4, w_out, xs, fw, xs.shape[0], False)
    qk_p = proj_p[:, :2 * E_C].reshape(bp, tp, 2 * E_C)
    qk_s = proj_s[:, :2 * E_C].reshape(bs, N_NEW, 2 * E_C)
    conv_p = jnp.concatenate([zeros(bp, CONV_W - 1, 2 * E_C), qk_p], axis=1)[:, tp:]
    conv_s = jnp.concatenate([conv_state, qk_s], axis=1)[:, N_NEW:]
    return yp, ys, (c_p, n_p, m_p, conv_p), (c_s, n_s, m_s, conv_s)


def kernel(x_prompt, x_sample, cache_fox_k, cache_fox_v, cache_fox_logf, cache_dil0_kv, cache_dil1_kv, cache_dil2_kv, state_mlstm_C, state_mlstm_n, state_mlstm_m, state_mlstm_conv, page_table, norm_w, final_norm_w, fox_w_in, fox_b_f, fox_w_out, dil_w_in, dil_w_out, mlstm_w_in, mlstm_b_i, mlstm_b_f, mlstm_conv_w, mlstm_conv_b, mlstm_norm_w, mlstm_w_out):
    bp, tp, d = x_prompt.shape
    bs, ts, _ = x_sample.shape
    assert ts == N_NEW and d == D_MODEL
    depth = norm_w.shape[0]
    dil_caches = (cache_dil0_kv, cache_dil1_kv, cache_dil2_kv)
    xp = x_prompt.reshape(bp * tp, d)
    xs = x_sample.reshape(bs * ts, d)
    fox_p, fox_s, dil_p, dil_s, ml_p, ml_s = [], [], [], [], [], []
    for i in range(depth):
        j, kind = divmod(i, 3)
        final = i == depth - 1
        if kind == 0:
            xp, xs, new_p, new_s = fox_layer(xp, xs, norm_w[i], fox_w_in[j], fox_b_f[j], fox_w_out[j],
                                             cache_fox_k[j], cache_fox_v[j], cache_fox_logf[j], page_table,
                                             bp, tp, bs, final_norm_w, final)
            fox_p.append(new_p)
            fox_s.append(new_s)
        elif kind == 1:
            assert not final
            xp, xs, new_p, new_s = dil_layer(xp, xs, norm_w[i], dil_w_in[j], dil_w_out[j],
                                             tuple(c[j] for c in dil_caches), bp, tp, bs)
            dil_p.append(new_p)
            dil_s.append(new_s)
        else:
            assert not final
            xp, xs, new_p, new_s = mlstm_layer(xp, xs, norm_w[i], mlstm_w_in[j], mlstm_b_i[j], mlstm_b_f[j],
                                               mlstm_conv_w[j], mlstm_conv_b[j], mlstm_norm_w[j], mlstm_w_out[j],
                                               state_mlstm_conv[j], state_mlstm_C[j], state_mlstm_n[j],
                                               state_mlstm_m[j], bp, tp, bs, final_norm_w)
            ml_p.append(new_p)
            ml_s.append(new_s)
    stk = lambda lst, f: jnp.stack([e[f] for e in lst], axis=0)
    return (xp.reshape(bp, tp, d), xs.reshape(bs, ts, d),
            stk(fox_p, 0), stk(fox_p, 1), stk(fox_p, 2),
            stk(fox_s, 0), stk(fox_s, 1), stk(fox_s, 2),
            stk(dil_p, 0), stk(dil_p, 1), stk(dil_p, 2),
            stk(dil_s, 0), stk(dil_s, 1), stk(dil_s, 2),
            stk(ml_p, 0), stk(ml_p, 1), stk(ml_p, 2), stk(ml_p, 3),
            stk(ml_s, 0), stk(ml_s, 1), stk(ml_s, 2), stk(ml_s, 3))
```

```python
import functools

import numpy as np
import jax
import jax.numpy as jnp
from jax import lax
from jax.experimental import pallas as pl
from jax.experimental.pallas import tpu as pltpu

F32 = jnp.float32
BF16 = jnp.bfloat16
NEG = -1e30
EPS = 1e-6

D_MODEL = 1024
PAGE = 128
N_HEAD_A = 16
HD_A = 64
N_HEAD_B = 16
HD_B = 64
DIL_GROUPS = ((128, 1), (512, 4), (2048, 16))
BAND = 128
N_HEAD_C = 4
E_C = 2 * D_MODEL
DH_C = E_C // N_HEAD_C
CONV_W = 4
N_NEW = 4
ROWS = N_NEW * N_HEAD_A
VMEM_LIMIT = 56 * 1024 * 1024


def _cparams(sem):
    return pltpu.CompilerParams(dimension_semantics=sem, vmem_limit_bytes=VMEM_LIMIT)


def _split3(x):
    hi = x.astype(BF16)
    r = x - hi.astype(F32)
    mid = r.astype(BF16)
    lo = (r - mid.astype(F32)).astype(BF16)
    return hi, mid, lo


def _dot(a, b):
    return jnp.dot(a, b, preferred_element_type=F32)


def _dot_nt(a, b):
    return lax.dot_general(a, b, (((1,), (1,)), ((), ())), preferred_element_type=F32)


def _dot_tn(a, b):
    return lax.dot_general(a, b, (((0,), (0,)), ((), ())), preferred_element_type=F32)


def _dot3_left(x, sel):
    hi, mid, lo = _split3(x)
    return _dot(hi, sel) + _dot(mid, sel) + _dot(lo, sel)


def _dot3_right(sel, x):
    hi, mid, lo = _split3(x)
    return _dot(sel, hi) + _dot(sel, mid) + _dot(sel, lo)


def _log_sigmoid(x):
    return jnp.minimum(x, 0.0) - jnp.log1p(jnp.exp(-jnp.abs(x)))


def _silu(z):
    return z / (1.0 + jnp.exp(-z))


def _sigmoid(z):
    return 1.0 / (1.0 + jnp.exp(-z))


def _head_mask(n_head, hd):
    h_idx = np.arange(n_head * hd) // hd
    return (h_idx[None, :] == np.arange(n_head)[:, None]).astype(np.float32)


def _norm_proj_kernel(x_ref, nw_ref, w_ref, wgh_ref, wgl_ref, o_ref, g_ref, h_ref):
    @pl.when(pl.program_id(1) == 0)
    def _():
        x = x_ref[...]
        h = x * lax.rsqrt(jnp.mean(x * x, axis=-1, keepdims=True) + EPS) * nw_ref[...]
        hb = h.astype(BF16)
        hl = (h - hb.astype(F32)).astype(BF16)
        h_ref[...] = hb
        g_ref[...] = _dot(hb, wgh_ref[...]) + _dot(hb, wgl_ref[...]) + _dot(hl, wgh_ref[...])

    o_ref[...] = _dot(h_ref[...], w_ref[...])


def norm_proj(x, nw, w, wg, tm, tn):
    n, d = x.shape
    nout = w.shape[1]
    wgp = jnp.zeros((d, 128), F32).at[:, : wg.shape[1]].set(wg)
    wgh = wgp.astype(BF16)
    wgl = (wgp - wgh.astype(F32)).astype(BF16)
    return pl.pallas_call(
        _norm_proj_kernel,
        grid=(n // tm, nout // tn),
        in_specs=[
            pl.BlockSpec((tm, d), lambda i, j: (i, 0)),
            pl.BlockSpec((1, d), lambda i, j: (0, 0)),
            pl.BlockSpec((d, tn), lambda i, j: (0, j)),
            pl.BlockSpec((d, 128), lambda i, j: (0, 0)),
            pl.BlockSpec((d, 128), lambda i, j: (0, 0)),
        ],
        out_specs=[
            pl.BlockSpec((tm, tn), lambda i, j: (i, j)),
            pl.BlockSpec((tm, 128), lambda i, j: (i, 0)),
        ],
        out_shape=[jax.ShapeDtypeStruct((n, nout), F32), jax.ShapeDtypeStruct((n, 128), F32)],
        scratch_shapes=[pltpu.VMEM((tm, d), BF16)],
        compiler_params=_cparams(("parallel", "arbitrary")),
        name="norm_proj",
    )(x, nw.reshape(1, d), w.astype(BF16), wgh, wgl)


def _out_proj_kernel(a_ref, z_ref, w_ref, x_ref, fw_ref, o_ref, *, final_norm):
    u = (a_ref[...] * _silu(z_ref[...])).astype(BF16)
    y = x_ref[...] + _dot(u, w_ref[...])
    if final_norm:
        y = y * lax.rsqrt(jnp.mean(y * y, axis=-1, keepdims=True) + EPS) * fw_ref[...]
    o_ref[...] = y


def out_proj(a, a_col, z, z_col, w, x, fw, tm, final_norm):
    n, d = x.shape
    e = w.shape[0]
    return pl.pallas_call(
        functools.partial(_out_proj_kernel, final_norm=final_norm),
        grid=(n // tm,),
        in_specs=[
            pl.BlockSpec((tm, e), lambda i: (i, a_col)),
            pl.BlockSpec((tm, e), lambda i: (i, z_col)),
            pl.BlockSpec((e, d), lambda i: (0, 0)),
            pl.BlockSpec((tm, d), lambda i: (i, 0)),
            pl.BlockSpec((1, d), lambda i: (0, 0)),
        ],
        out_specs=pl.BlockSpec((tm, d), lambda i: (i, 0)),
        out_shape=jax.ShapeDtypeStruct((n, d), F32),
        compiler_params=_cparams(("parallel",)),
        name="out_proj",
    )(a, z, w.astype(BF16), x, fw.reshape(1, d))


def _dil_out_proj_kernel(o0_ref, o1_ref, o2_ref, l0_ref, l1_ref, l2_ref, z_ref, w_ref, x_ref, o_ref):
    l0, l1, l2 = l0_ref[...], l1_ref[...], l2_ref[...]
    m = jnp.maximum(jnp.maximum(l0, l1), l2)
    e0, e1, e2 = jnp.exp(l0 - m), jnp.exp(l1 - m), jnp.exp(l2 - m)
    o = (e0 * o0_ref[...] + e1 * o1_ref[...] + e2 * o2_ref[...]) / (e0 + e1 + e2)
    u = (o * _silu(z_ref[...])).astype(BF16)
    o_ref[...] = x_ref[...] + _dot(u, w_ref[...])


def dil_out_proj(os_, ls_, z, z_col, w, x, tm):
    n, d = x.shape
    e = w.shape[0]
    blk = pl.BlockSpec((tm, e), lambda i: (i, 0))
    return pl.pallas_call(
        _dil_out_proj_kernel,
        grid=(n // tm,),
        in_specs=[blk] * 6 + [
            pl.BlockSpec((tm, e), lambda i: (i, z_col)),
            pl.BlockSpec((e, d), lambda i: (0, 0)),
            pl.BlockSpec((tm, d), lambda i: (i, 0)),
        ],
        out_specs=pl.BlockSpec((tm, d), lambda i: (i, 0)),
        out_shape=jax.ShapeDtypeStruct((n, d), F32),
        compiler_params=_cparams(("parallel",)),
        name="dil_out_proj",
    )(*os_, *ls_, z, w.astype(BF16), x)


def _fox_proj_kernel(x_ref, nw_ref, w_ref, wgh_ref, wgl_ref, q_ref, kt_ref, vt_ref, z_ref, g_ref, h_ref):
    j = pl.program_id(1)

    @pl.when(j == 0)
    def _():
        x = x_ref[...]
        h = x * lax.rsqrt(jnp.mean(x * x, axis=-1, keepdims=True) + EPS) * nw_ref[...]
        hb = h.astype(BF16)
        hl = (h - hb.astype(F32)).astype(BF16)
        h_ref[...] = hb
        g_ref[...] = _dot_nt(hb, wgh_ref[...]) + _dot_nt(hb, wgl_ref[...]) + _dot_nt(hl, wgh_ref[...])
        q_ref[...] = _dot_nt(hb, w_ref[0])

    @pl.when(j == 1)
    def _():
        kt_ref[0] = _dot_nt(w_ref[0], h_ref[...])

    @pl.when(j == 2)
    def _():
        vt_ref[0] = _dot_nt(w_ref[0], h_ref[...])

    @pl.when(j == 3)
    def _():
        z_ref[...] = _dot_nt(h_ref[...], w_ref[0])


def fox_proj(x, nw, w_in, bsz, t, tm):
    n, d = x.shape
    e = N_HEAD_A * HD_A
    nblk = t // tm
    wt = w_in.T
    w4 = jnp.stack([wt[:e], wt[e:2 * e], wt[2 * e:3 * e], wt[3 * e + N_HEAD_A:]], axis=0).astype(BF16)
    wg = jnp.zeros((128, d), F32).at[:N_HEAD_A].set(wt[3 * e:3 * e + N_HEAD_A])
    wgh = wg.astype(BF16)
    wgl = (wg - wgh.astype(F32)).astype(BF16)
    row_blk = pl.BlockSpec((tm, e), lambda i, j: (i, 0))
    t_blk = pl.BlockSpec((1, e, tm), lambda i, j: (i // nblk, 0, i % nblk))
    return pl.pallas_call(
        _fox_proj_kernel,
        grid=(n // tm, 4),
        in_specs=[
            pl.BlockSpec((tm, d), lambda i, j: (i, 0)),
            pl.BlockSpec((1, d), lambda i, j: (0, 0)),
            pl.BlockSpec((1, e, d), lambda i, j: (j, 0, 0)),
            pl.BlockSpec((128, d), lambda i, j: (0, 0)),
            pl.BlockSpec((128, d), lambda i, j: (0, 0)),
        ],
        out_specs=[row_blk, t_blk, t_blk, row_blk, pl.BlockSpec((tm, 128), lambda i, j: (i, 0))],
        out_shape=[jax.ShapeDtypeStruct((n, e), F32), jax.ShapeDtypeStruct((bsz, e, t), F32),
                   jax.ShapeDtypeStruct((bsz, e, t), F32), jax.ShapeDtypeStruct((n, e), F32),
                   jax.ShapeDtypeStruct((n, 128), F32)],
        scratch_shapes=[pltpu.VMEM((tm, d), BF16)],
        compiler_params=_cparams(("parallel", "arbitrary")),
        name="fox_proj",
    )(x, nw.reshape(1, d), w4, wgh, wgl)


AUG_Q_C = (0, 1, 2)
AUG_K_C = (3, 4, 5)
AUG_PER_HEAD = 6
ONES_LANE = N_HEAD_A


def _fox_sel_matrices():
    selq = np.zeros((3, 128, D_MODEL), np.float32)
    selk = np.zeros((3, 128, D_MODEL), np.float32)
    for hp in range(N_HEAD_A // 2):
        for hh in range(2):
            base = hp * 128 + hh * AUG_PER_HEAD
            for p in range(3):
                selq[p, 2 * hp + hh, base + AUG_Q_C[p]] = 1.0
                selk[p, 2 * hp + hh, base + AUG_K_C[p]] = -1.0
                selq[0, ONES_LANE, base + AUG_K_C[p]] = 1.0
                selk[0, ONES_LANE, base + AUG_Q_C[p]] = 1.0
    return selq, np.ascontiguousarray(selk.transpose(0, 2, 1))


def _fox_prep_kernel(q_ref, kt_ref, vt_ref, g_ref, bf_ref, selq_ref, selkt_ref,
                     qb_ref, qa_ref, kbt_ref, kat_ref, vbt_ref, lft_ref, carry_ref, *, tb):
    @pl.when(pl.program_id(1) == 0)
    def _():
        carry_ref[...] = jnp.zeros_like(carry_ref)

    lf = _log_sigmoid(g_ref[...] + bf_ref[...])
    lft_ref[0] = lf.T[:N_HEAD_A]
    row = lax.broadcasted_iota(jnp.int32, (tb, tb), 0)
    col = lax.broadcasted_iota(jnp.int32, (tb, tb), 1)
    tril = (col <= row).astype(BF16)
    c = _dot3_right(tril, lf) + carry_ref[...]
    carry_ref[...] = c[tb - 1:tb, :]
    lane = lax.broadcasted_iota(jnp.int32, c.shape, 1)
    c1 = jnp.where(lane == ONES_LANE, 1.0, c)
    parts = _split3(c1)
    parts_t = _split3(c1.T)
    qa = _dot(parts[0], selq_ref[0])
    kat = _dot(selkt_ref[0], parts_t[0])
    for p in range(1, 3):
        qa = qa + _dot(parts[p], selq_ref[p])
        kat = kat + _dot(selkt_ref[p], parts_t[p])
    qa_ref[...] = qa.astype(BF16)
    kat_ref[0] = kat.astype(BF16)
    qb_ref[...] = (q_ref[...] * (HD_A ** -0.5)).astype(BF16)
    kbt_ref[0] = kt_ref[0].astype(BF16)
    vbt_ref[0] = vt_ref[0].astype(BF16)


def fox_prep(q, kt, vt, gates, b_f, bsz, t, tb):
    n = bsz * t
    nb = t // tb
    e = N_HEAD_A * HD_A
    selq, selkt = _fox_sel_matrices()
    bfp = jnp.zeros((1, 128), F32).at[0, :N_HEAD_A].set(b_f)
    row_blk = pl.BlockSpec((tb, e), lambda b, i: (b * nb + i, 0))
    t_blk = pl.BlockSpec((1, e, tb), lambda b, i: (b, 0, i))
    g_blk = pl.BlockSpec((tb, 128), lambda b, i: (b * nb + i, 0))
    return pl.pallas_call(
        functools.partial(_fox_prep_kernel, tb=tb),
        grid=(bsz, nb),
        in_specs=[row_blk, t_blk, t_blk, g_blk, pl.BlockSpec((1, 128), lambda b, i: (0, 0)),
                  pl.BlockSpec((3, 128, e), lambda b, i: (0, 0, 0)),
                  pl.BlockSpec((3, e, 128), lambda b, i: (0, 0, 0))],
        out_specs=[row_blk, row_blk, t_blk, t_blk, t_blk,
                   pl.BlockSpec((1, N_HEAD_A, tb), lambda b, i: (b, 0, i))],
        out_shape=[jax.ShapeDtypeStruct((n, e), BF16)] * 2 + [jax.ShapeDtypeStruct((bsz, e, t), BF16)] * 3
        + [jax.ShapeDtypeStruct((bsz, N_HEAD_A, t), F32)],
        scratch_shapes=[pltpu.VMEM((1, 128), F32)],
        compiler_params=_cparams(("parallel", "arbitrary")),
        name="fox_prep",
    )(q, kt, vt, gates, bfp, jnp.asarray(selq, BF16), jnp.asarray(selkt, BF16))


def _fox_attn_kernel(qi_ref, kj_ref, qb_ref, qa_ref, kbt_ref, kat_ref, vbt_ref, o_ref,
                     lhs_ref, m_ref, acc_ref, *, tq):
    s_idx = pl.program_id(2)
    qi = qi_ref[s_idx]
    kj = kj_ref[s_idx]

    @pl.when(kj == 0)
    def _init():
        qcat = jnp.concatenate([qb_ref[...], qa_ref[...]], axis=1)
        lane = lax.broadcasted_iota(jnp.int32, (1, 256), 1)
        for hh in range(2):
            lo = 128 + hh * AUG_PER_HEAD
            keep = ((lane >= hh * 64) & (lane < hh * 64 + 64)) | ((lane >= lo) & (lane < lo + AUG_PER_HEAD))
            lhs_ref[hh] = jnp.where(keep, qcat, jnp.zeros_like(qcat))
        m_ref[...] = jnp.full_like(m_ref, NEG)
        acc_ref[...] = jnp.zeros_like(acc_ref)

    def step(diagonal):
        kcat = jnp.concatenate([kbt_ref[0], kat_ref[0]], axis=0)
        v = vbt_ref[0]
        vrow = lax.broadcasted_iota(jnp.int32, v.shape, 0)
        for hh in range(2):
            vh = jnp.where((vrow >= hh * 64) & (vrow < hh * 64 + 64), v, jnp.ones_like(v))
            s = _dot(lhs_ref[hh], kcat)
            if diagonal:
                row = lax.broadcasted_iota(jnp.int32, s.shape, 0)
                col = lax.broadcasted_iota(jnp.int32, s.shape, 1)
                s = jnp.where(col <= row, s, NEG)
            m_prev = m_ref[hh]
            m_new = jnp.maximum(m_prev, jnp.max(s, axis=1, keepdims=True))
            p = jnp.exp(s - m_new)
            acc_ref[hh] = jnp.exp(m_prev - m_new) * acc_ref[hh] + _dot_nt(p.astype(BF16), vh)
            m_ref[hh] = m_new

    @pl.when(kj < qi)
    def _off():
        step(False)

    @pl.when(kj == qi)
    def _diag():
        step(True)
        lane = lax.broadcasted_iota(jnp.int32, (tq, 128), 1)
        a0, a1 = acc_ref[0], acc_ref[1]
        o_ref[...] = jnp.where(lane < 64, a0 / pltpu.roll(a0, 64, 1), a1 / pltpu.roll(a1, 64, 1))


def fox_attention_prompt(qb, qa, kbt, kat, vbt, bsz, t, tq):
    n = bsz * t
    nq = t // tq
    qi = np.concatenate([np.full(i + 1, i, np.int32) for i in range(nq)])
    kj = np.concatenate([np.arange(i + 1, dtype=np.int32) for i in range(nq)])
    q_blk = pl.BlockSpec((tq, 128), lambda b, hp, s, qi_r, kj_r: (b * nq + qi_r[s], hp))
    k_blk = pl.BlockSpec((1, 128, tq), lambda b, hp, s, qi_r, kj_r: (b, hp, kj_r[s]))
    return pl.pallas_call(
        functools.partial(_fox_attn_kernel, tq=tq),
        grid_spec=pltpu.PrefetchScalarGridSpec(
            num_scalar_prefetch=2,
            grid=(bsz, N_HEAD_A // 2, len(qi)),
            in_specs=[q_blk, q_blk, k_blk, k_blk, k_blk],
            out_specs=q_blk,
            scratch_shapes=[pltpu.VMEM((2, tq, 256), BF16), pltpu.VMEM((2, tq, 1), F32),
                            pltpu.VMEM((2, tq, 128), F32)],
        ),
        out_shape=jax.ShapeDtypeStruct((n, D_MODEL), F32),
        compiler_params=_cparams(("parallel", "parallel", "arbitrary")),
        name="fox_attn",
    )(jnp.asarray(qi), jnp.asarray(kj), qb, qa, kbt, kat, vbt)


def _flash_update(q_ref, kt, vt, bias, valid, m_ref, l_ref, acc_ref):
    s = _dot(q_ref[0], kt.astype(BF16)) + bias
    if valid is not None:
        s = jnp.where(valid, s, NEG)
    m_prev = m_ref[...]
    m_new = jnp.maximum(m_prev, jnp.max(s, axis=1, keepdims=True))
    alpha = jnp.exp(m_prev - m_new)
    pr = jnp.exp(s - m_new)
    l_ref[...] = alpha * l_ref[...] + jnp.sum(pr, axis=1, keepdims=True)
    acc_ref[...] = alpha * acc_ref[...] + _dot_nt(pr.astype(BF16), vt.astype(BF16))
    m_ref[...] = m_new


def _head_diag(x, hmask):
    n_head = hmask.shape[0]
    return jnp.sum(x.reshape(N_NEW, n_head, x.shape[1]) * hmask[None], axis=1)


def _fox_sample_kernel(pt_ref, q_ref, *refs, n_groups, gp):
    k_refs, v_refs, lf_refs = refs[:gp], refs[gp:2 * gp], refs[2 * gp:3 * gp]
    (kn_ref, vn_ref, gn_ref, bf_ref, hmask_ref, o_ref, lfn_ref,
     m_ref, l_ref, acc_ref, carry_ref, ecol_ref) = refs[3 * gp:]
    p = pl.program_id(1)
    w = gp * PAGE

    @pl.when(p == 0)
    def _new_tokens():
        m_ref[...] = jnp.full_like(m_ref, NEG)
        l_ref[...] = jnp.zeros_like(l_ref)
        acc_ref[...] = jnp.zeros_like(acc_ref)
        carry_ref[...] = jnp.zeros_like(carry_ref)
        lfn = _log_sigmoid(gn_ref[0] + bf_ref[...])
        lfn_ref[0] = lfn
        r_i = lax.broadcasted_iota(jnp.int32, (128, 128), 0)
        c_i = lax.broadcasted_iota(jnp.int32, (128, 128), 1)
        incl = ((r_i <= c_i) & (r_i < N_NEW)).astype(BF16)
        e = _dot3_left(lfn, incl)
        e4 = jnp.concatenate([e] * N_NEW, axis=0)
        tok = lax.broadcasted_iota(jnp.int32, (ROWS, 128), 0) // N_HEAD_A
        lane = lax.broadcasted_iota(jnp.int32, (ROWS, 128), 1)
        ecol = jnp.sum(jnp.where(lane == tok, e4, 0.0), axis=1, keepdims=True)
        ecol_ref[...] = ecol
        valid = (lane <= tok) & (lane < N_NEW)
        _flash_update(q_ref, kn_ref[0], vn_ref[0], ecol - e4, valid, m_ref, l_ref, acc_ref)

    @pl.when(p > 0)
    def _pages():
        cat = lambda rs: jnp.concatenate([r[0, 0] for r in rs], axis=1) if gp > 1 else rs[0][0, 0]
        lf = cat(lf_refs)
        r_i = lax.broadcasted_iota(jnp.int32, (w, w), 0)
        c_i = lax.broadcasted_iota(jnp.int32, (w, w), 1)
        after = (r_i > c_i).astype(BF16)
        d = _dot3_left(lf, after) + carry_ref[...]
        carry_ref[...] = carry_ref[...] + jnp.sum(lf, axis=1, keepdims=True)
        d4 = jnp.concatenate([d] * N_NEW, axis=0)
        _flash_update(q_ref, cat(k_refs), cat(v_refs), ecol_ref[...] + d4, None, m_ref, l_ref, acc_ref)

    @pl.when(p == n_groups)
    def _finish():
        o_ref[0] = _head_diag(acc_ref[...] / l_ref[...], hmask_ref[...])


FOX_PAGES_PER_STEP = 4


def fox_attention_sample(qbd, knt, vnt, gnt, b_f, cache_kt, cache_vt, cache_lft, layer, page_table):
    bs, n_pages = page_table.shape
    gp = FOX_PAGES_PER_STEP
    assert n_pages % gp == 0
    n_groups = n_pages // gp
    e = N_HEAD_A * HD_A

    def page_map(c):
        def index_map(b, p, pt):
            grp = n_groups - jnp.maximum(p, 1)
            return (layer, pt[b * n_pages + grp * gp + c], 0, 0)
        return index_map

    per_b = lambda shape: pl.BlockSpec(shape, lambda b, p, pt: (b, 0, 0))
    const2 = lambda shape: pl.BlockSpec(shape, lambda b, p, pt: (0, 0))
    kv_specs = [pl.BlockSpec((1, 1, e, PAGE), page_map(c)) for c in range(gp)]
    lf_specs = [pl.BlockSpec((1, 1, N_HEAD_A, PAGE), page_map(c)) for c in range(gp)]
    return pl.pallas_call(
        functools.partial(_fox_sample_kernel, n_groups=n_groups, gp=gp),
        grid_spec=pltpu.PrefetchScalarGridSpec(
            num_scalar_prefetch=1,
            grid=(bs, n_groups + 1),
            in_specs=[per_b((1, ROWS, e))] + kv_specs + kv_specs + lf_specs + [
                per_b((1, e, 128)), per_b((1, e, 128)), per_b((1, N_HEAD_A, 128)),
                const2((N_HEAD_A, 1)), const2((N_HEAD_A, e))],
            out_specs=[per_b((1, N_NEW, e)), per_b((1, N_HEAD_A, 128))],
            scratch_shapes=[pltpu.VMEM((ROWS, 1), F32), pltpu.VMEM((ROWS, 1), F32),
                            pltpu.VMEM((ROWS, e), F32), pltpu.VMEM((N_HEAD_A, 1), F32),
                            pltpu.VMEM((ROWS, 1), F32)],
        ),
        out_shape=[jax.ShapeDtypeStruct((bs, N_NEW, e), F32),
                   jax.ShapeDtypeStruct((bs, N_HEAD_A, 128), F32)],
        compiler_params=_cparams(("parallel", "arbitrary")),
        name="fox_sample",
    )(page_table.reshape(-1), qbd, *([cache_kt] * gp), *([cache_vt] * gp), *([cache_lft] * gp),
      knt, vnt, gnt, b_f.reshape(N_HEAD_A, 1), jnp.asarray(_head_mask(N_HEAD_A, HD_A)))


def _alibi_slope(h):
    return float(2.0 ** (-8.0 * (h + 1.0) / N_HEAD_B))


def _dil_sample_kernel(q_ref, k_ref, v_ref, kn_ref, vn_ref, slope_ref, hmask_ref, o_ref, lse_ref,
                       m_ref, l_ref, acc_ref, *, dil, window, cw, n_chunks):
    p = pl.program_id(1)
    tok = lax.broadcasted_iota(jnp.int32, (ROWS, cw), 0) // N_HEAD_B
    lane = lax.broadcasted_iota(jnp.int32, (ROWS, cw), 1)
    slope = slope_ref[...]

    @pl.when(p == 0)
    def _new_tokens():
        m_ref[...] = jnp.full_like(m_ref, NEG)
        l_ref[...] = jnp.zeros_like(l_ref)
        acc_ref[...] = jnp.zeros_like(acc_ref)
        tok_n = lax.broadcasted_iota(jnp.int32, (ROWS, 128), 0) // N_HEAD_B
        lane_n = lax.broadcasted_iota(jnp.int32, (ROWS, 128), 1)
        dist = tok_n - lane_n
        valid = (dist >= 0) & ((dist & (dil - 1)) == 0) & (lane_n < N_NEW)
        _flash_update(q_ref, kn_ref[0], vn_ref[0], -slope * dist.astype(F32), valid, m_ref, l_ref, acc_ref)

    @pl.when(p > 0)
    def _chunk():
        dist = window + tok - ((p - 1) * cw + lane)
        valid = ((dist & (dil - 1)) == 0) & (dist <= window)
        _flash_update(q_ref, k_ref[0, 0], v_ref[0, 0], -slope * dist.astype(F32), valid, m_ref, l_ref, acc_ref)

    @pl.when(p == n_chunks)
    def _finish():
        hmask = hmask_ref[...]
        l = l_ref[...]
        o_ref[0] = _head_diag(acc_ref[...] / l, hmask)
        lse = m_ref[...] + jnp.log(l)
        lse_ref[0] = _head_diag(jnp.broadcast_to(lse, acc_ref.shape), hmask)


def dil_sample_attention(qbd, knt, vnt, buf_t, g):
    window, dil = DIL_GROUPS[g]
    assert dil & (dil - 1) == 0 and buf_t.shape[3] == window
    bs = qbd.shape[0]
    e = N_HEAD_B * HD_B
    cw = min(window, 512)
    n_chunks = window // cw
    slope = np.array([[_alibi_slope(r % N_HEAD_B)] for r in range(ROWS)], np.float32)
    per_b = lambda shape: pl.BlockSpec(shape, lambda b, p: (b, 0, 0))
    chunk = lambda kv: pl.BlockSpec((1, 1, e, cw), lambda b, p, kv=kv: (b, kv, 0, jnp.maximum(p - 1, 0)))
    return pl.pallas_call(
        functools.partial(_dil_sample_kernel, dil=dil, window=window, cw=cw, n_chunks=n_chunks),
        grid=(bs, n_chunks + 1),
        in_specs=[per_b((1, ROWS, e)), chunk(0), chunk(1), per_b((1, e, 128)), per_b((1, e, 128)),
                  pl.BlockSpec((ROWS, 1), lambda b, p: (0, 0)),
                  pl.BlockSpec((N_HEAD_B, e), lambda b, p: (0, 0))],
        out_specs=[per_b((1, N_NEW, e))] * 2,
        out_shape=[jax.ShapeDtypeStruct((bs, N_NEW, e), F32)] * 2,
        scratch_shapes=[pltpu.VMEM((ROWS, 1), F32), pltpu.VMEM((ROWS, 1), F32), pltpu.VMEM((ROWS, e), F32)],
        compiler_params=_cparams(("parallel", "arbitrary")),
        name="dil_sample%d" % g,
    )(qbd, buf_t, buf_t, knt, vnt, jnp.asarray(slope), jnp.asarray(_head_mask(N_HEAD_B, HD_B)))


def _block_diag_queries(q, n_head, hd):
    bs = q.shape[0]
    qbd = (q * (hd ** -0.5))[:, :, None, :] * jnp.asarray(_head_mask(n_head, hd))[None, None]
    return qbd.reshape(bs, N_NEW * n_head, n_head * hd).astype(BF16)


def _new_token_major(x_t, bs):
    f = x_t.shape[0]
    return jnp.pad(x_t.reshape(f, bs, N_NEW).transpose(1, 0, 2), ((0, 0), (0, 0), (0, 128 - N_NEW)))


def _dil_band_kernel(q_ref, kp_ref, kc_ref, vp_ref, vc_ref, slope_ref, o_ref, lse_ref, *, dil):
    n = pl.program_id(1)
    r = BAND
    qrow = lax.broadcasted_iota(jnp.int32, (r, 2 * r), 0)
    kcol = lax.broadcasted_iota(jnp.int32, (r, 2 * r), 1)
    dist = r + qrow - kcol
    valid = (dist >= 0) & (dist <= r) & ((kcol >= r) | (n > 0))
    distf = (dist * dil).astype(F32)
    lane = lax.broadcasted_iota(jnp.int32, (r, 128), 1)
    first = lane < HD_B

    def residue(res):
        rows = pl.ds(res, r, stride=dil) if dil > 1 else pl.ds(0, r)
        q = q_ref[rows, :] * (HD_B ** -0.5)
        k2 = jnp.concatenate([kp_ref[rows, :], kc_ref[rows, :]], axis=0).astype(BF16)
        v2 = jnp.concatenate([vp_ref[rows, :], vc_ref[rows, :]], axis=0).astype(BF16)
        outs, lses = [], []
        for hh in range(2):
            qh = jnp.where(first == (hh == 0), q, 0.0).astype(BF16)
            s = _dot_nt(qh, k2) - slope_ref[0, hh:hh + 1, 0:1] * distf
            s = jnp.where(valid, s, NEG)
            m = jnp.max(s, axis=1, keepdims=True)
            p = jnp.exp(s - m)
            l = jnp.sum(p, axis=1, keepdims=True)
            outs.append(_dot(p.astype(BF16), v2) / l)
            lses.append(m + jnp.log(l))
        o_ref[rows, :] = jnp.where(first, outs[0], outs[1])
        lse_ref[rows, :] = jnp.where(first, lses[0], lses[1])

    if dil == 1:
        residue(0)
    else:
        def body(res, carry):
            residue(res)
            return carry
        lax.fori_loop(0, dil, body, 0)


def dil_band_attention(proj, g, bsz, t):
    window, dil = DIL_GROUPS[g]
    assert window // dil == BAND
    ch = BAND * dil
    nch = t // ch
    npair = N_HEAD_B // 2
    slopes = np.zeros((npair, 8, 128), np.float32)
    for h in range(N_HEAD_B):
        slopes[h // 2, h % 2, :] = _alibi_slope(h)

    def cur(c):
        return pl.BlockSpec((ch, 128), lambda b, n, hp, c=c: (b * nch + n, c * npair + hp))

    def prev(c):
        return pl.BlockSpec((ch, 128), lambda b, n, hp, c=c: (b * nch + jnp.maximum(n - 1, 0), c * npair + hp))

    out_blk = pl.BlockSpec((ch, 128), lambda b, n, hp: (b * nch + n, hp))
    return pl.pallas_call(
        functools.partial(_dil_band_kernel, dil=dil),
        grid=(bsz, nch, npair),
        in_specs=[cur(3 * g), prev(3 * g + 1), cur(3 * g + 1), prev(3 * g + 2), cur(3 * g + 2),
                  pl.BlockSpec((1, 8, 128), lambda b, n, hp: (hp, 0, 0))],
        out_specs=[out_blk, out_blk],
        out_shape=[jax.ShapeDtypeStruct((bsz * t, D_MODEL), F32)] * 2,
        compiler_params=_cparams(("parallel", "arbitrary", "arbitrary")),
        name="dil_band%d" % g,
    )(proj, proj, proj, proj, proj, jnp.asarray(slopes))


def _mlstm_kernel(qp_ref, kp_ref, v_ref, op_ref, g_ref, gb_ref, cwq_ref, cwk_ref, cbq_ref, cbk_ref,
                  nw_ref, csq_ref, csk_ref, c0_ref, n0_ref, m0_ref,
                  hn_ref, c_out_ref, n_out_ref, m_out_ref,
                  c_s, n_s, m_s, xq_s, xk_s, *, lc, n_valid, nc):
    h = pl.program_id(1)
    c = pl.program_id(2)

    @pl.when(c == 0)
    def _():
        c_s[...] = c0_ref[0, 0]
        n_s[...] = n0_ref[0, 0]
        m_s[...] = m0_ref[0, 0]
        xq_s[0:8, :] = csq_ref[0]
        xk_s[0:8, :] = csk_ref[0]

    def conv(x_ref, xs, cw_ref, cb_ref):
        xs[8:8 + lc, :] = x_ref[...]
        acc = cb_ref[...] + cw_ref[0:1, :] * xs[pl.ds(8 - (CONV_W - 1), lc), :]
        for w in range(1, CONV_W):
            acc = acc + cw_ref[w:w + 1, :] * xs[pl.ds(8 - (CONV_W - 1) + w, lc), :]
        xs[0:8, :] = xs[lc:lc + 8, :]
        return _silu(acc)

    q = conv(qp_ref, xq_s, cwq_ref, cbq_ref)
    k = conv(kp_ref, xk_s, cwk_ref, cbk_ref) * (DH_C ** -0.5)
    v = v_ref[...]

    g = g_ref[...] + gb_ref[...]
    lane = lax.broadcasted_iota(jnp.int32, g.shape, 1)
    li_col = jnp.sum(jnp.where(lane == h, g, 0.0), axis=1, keepdims=True)
    lf_col = _log_sigmoid(jnp.sum(jnp.where(lane == N_HEAD_C + h, g, 0.0), axis=1, keepdims=True))
    if n_valid < lc:
        rows = lax.broadcasted_iota(jnp.int32, (lc, 1), 0)
        li_col = jnp.where(rows < n_valid, li_col, NEG)
        lf_col = jnp.where(rows < n_valid, lf_col, 0.0)
    row = lax.broadcasted_iota(jnp.int32, (lc, lc), 0)
    col = lax.broadcasted_iota(jnp.int32, (lc, lc), 1)
    eye = row == col
    tril = col <= row
    to_row = lambda x_col: jnp.sum(jnp.where(eye, x_col, 0.0), axis=0, keepdims=True)
    lf_row = to_row(lf_col)
    li_row = to_row(li_col)
    b_col = jnp.sum(jnp.where(tril, lf_row, 0.0), axis=1, keepdims=True)
    b_row = to_row(b_col)
    m_prev = m_s[...]
    d = jnp.where(tril, b_col - b_row + li_row, NEG)
    m_t = jnp.maximum(b_col + m_prev, jnp.max(d, axis=1, keepdims=True))
    dexp = jnp.where(tril, jnp.exp(d - m_t), 0.0)
    inter = jnp.exp(b_col + m_prev - m_t)
    qb, kb, vb = q.astype(BF16), k.astype(BF16), v.astype(BF16)
    w = _dot_nt(qb, kb) * dexp
    num = _dot(w.astype(BF16), vb) + inter * _dot(qb, c_s[...].astype(BF16))
    den = jnp.sum(w, axis=1, keepdims=True) + inter * jnp.sum(q * n_s[...], axis=1, keepdims=True)
    hh = num / jnp.maximum(jnp.abs(den), jnp.exp(-m_t))
    m_new = m_t[lc - 1:lc, :]
    b_last = b_col[lc - 1:lc, :]
    g_col = jnp.exp(b_last - b_col + li_col - m_new)
    decay = jnp.exp(b_last + m_prev - m_new)
    kg = k * g_col
    c_s[...] = decay * c_s[...] + _dot_tn(kg.astype(BF16), vb)
    n_s[...] = decay * n_s[...] + jnp.sum(kg, axis=0, keepdims=True)
    m_s[...] = m_new

    ho = hh * _sigmoid(op_ref[...])
    mu = jnp.mean(ho, axis=1, keepdims=True)
    var = jnp.mean(jnp.square(ho - mu), axis=1, keepdims=True)
    hn_ref[...] = (ho - mu) * lax.rsqrt(var + EPS) * nw_ref[...]

    @pl.when(c == nc - 1)
    def _():
        c_out_ref[0, 0] = c_s[...]
        n_out_ref[0, 0] = n_s[...]
        m_out_ref[0, 0] = m_s[...]


def mlstm_mix(proj, gates, b_i, b_f, conv_w, conv_b, norm_w, conv_state, c0, n0, m0, bsz, t, lc, n_valid):
    n = bsz * t
    nc = t // lc
    assert n_valid == lc or nc == 1
    nh = N_HEAD_C
    gb = jnp.zeros((1, 128), F32).at[0, :nh].set(b_i).at[0, nh:2 * nh].set(b_f)
    cs = jnp.pad(conv_state, ((0, 0), (8 - (CONV_W - 1), 0), (0, 0)))
    blk = lambda c0_: pl.BlockSpec((lc, DH_C), lambda b, h, c, c0_=c0_: (b * nc + c, c0_ + h))
    per_h = lambda rows, c0_: pl.BlockSpec((rows, DH_C), lambda b, h, c, c0_=c0_: (0, c0_ + h))
    st4 = lambda shape: pl.BlockSpec(shape, lambda b, h, c: (b, h, 0, 0))
    hn, c_out, n_out, m_out = pl.pallas_call(
        functools.partial(_mlstm_kernel, lc=lc, n_valid=n_valid, nc=nc),
        grid=(bsz, nh, nc),
        in_specs=[
            blk(0), blk(nh), blk(2 * nh), blk(3 * nh),
            pl.BlockSpec((lc, 128), lambda b, h, c: (b * nc + c, 0)),
            pl.BlockSpec((1, 128), lambda b, h, c: (0, 0)),
            per_h(CONV_W, 0), per_h(CONV_W, nh), per_h(1, 0), per_h(1, nh),
            per_h(1, 0),
            pl.BlockSpec((1, 8, DH_C), lambda b, h, c: (b, 0, h)),
            pl.BlockSpec((1, 8, DH_C), lambda b, h, c: (b, 0, nh + h)),
            st4((1, 1, DH_C, DH_C)), st4((1, 1, 1, DH_C)), st4((1, 1, 1, 1)),
        ],
        out_specs=[
            pl.BlockSpec((lc, DH_C), lambda b, h, c: (b * nc + c, h)),
            st4((1, 1, DH_C, DH_C)), st4((1, 1, 1, DH_C)), st4((1, 1, 1, 1)),
        ],
        out_shape=[
            jax.ShapeDtypeStruct((n, E_C), F32),
            jax.ShapeDtypeStruct((bsz, nh, DH_C, DH_C), F32),
            jax.ShapeDtypeStruct((bsz, nh, 1, DH_C), F32),
            jax.ShapeDtypeStruct((bsz, nh, 1, 1), F32),
        ],
        scratch_shapes=[pltpu.VMEM((DH_C, DH_C), F32), pltpu.VMEM((1, DH_C), F32), pltpu.VMEM((1, 1), F32),
                        pltpu.VMEM((lc + 8, DH_C), F32), pltpu.VMEM((lc + 8, DH_C), F32)],
        compiler_params=_cparams(("parallel", "parallel", "arbitrary")),
        name="mlstm",
    )(proj, proj, proj, proj, gates, gb, conv_w, conv_w, conv_b.reshape(1, -1), conv_b.reshape(1, -1),
      norm_w.reshape(1, -1), cs, cs, c0, n0.reshape(bsz, nh, 1, DH_C), m0.reshape(bsz, nh, 1, 1))
    return hn, c_out, n_out.reshape(bsz, nh, DH_C), m_out.reshape(bsz, nh)


TM_PROMPT = 512
TN_PROJ = 512
FOX_TQ = 1024
FOX_PREP_TB = 512
MLSTM_LC = 256
SAMPLE_LC = 16


def fox_layer(xp, xs, nw, w_in, b_f, w_out, cache_kt, cache_vt, cache_lft, layer, page_table, bp, tp, bs, fw,
              final):
    e = N_HEAD_A * HD_A
    n_s = bs * N_NEW
    q_p, kt_p, vt_p, z_p, gates_p = fox_proj(xp, nw, w_in, bp, tp, TM_PROMPT)
    q_s, kt_s, vt_s, z_s, gates_s = fox_proj(xs, nw, w_in, 1, n_s, n_s)
    qb, qa, kbt, kat, vbt, lft_p = fox_prep(q_p, kt_p, vt_p, gates_p, b_f, bp, tp, FOX_PREP_TB)
    o_p = fox_attention_prompt(qb, qa, kbt, kat, vbt, bp, tp, FOX_TQ)
    yp = out_proj(o_p, 0, z_p, 0, w_out, xp, fw, TM_PROMPT, final)
    qbd = _block_diag_queries(q_s.reshape(bs, N_NEW, e), N_HEAD_A, HD_A)
    gnt = _new_token_major(gates_s[:, :N_HEAD_A].T, bs)
    o_s, lfn = fox_attention_sample(qbd, _new_token_major(kt_s[0], bs), _new_token_major(vt_s[0], bs), gnt, b_f,
                                    cache_kt, cache_vt, cache_lft, layer, page_table)
    ys = out_proj(o_s.reshape(n_s, e), 0, z_s, 0, w_out, xs, fw, n_s, final)
    heads_t = lambda a_t, b_, t_: a_t.reshape(b_, N_HEAD_A, HD_A, t_).transpose(0, 3, 1, 2)
    new_p = (heads_t(kt_p, bp, tp), heads_t(vt_p, bp, tp), lft_p.transpose(0, 2, 1))
    new_s = (kt_s[0].T.reshape(bs, N_NEW, N_HEAD_A, HD_A), vt_s[0].T.reshape(bs, N_NEW, N_HEAD_A, HD_A),
             lfn[:, :, :N_NEW].transpose(0, 2, 1))
    return yp, ys, new_p, new_s


def dil_layer(xp, xs, nw, w_in, w_out, caches, bp, tp, bs):
    e = N_HEAD_B * HD_B
    no_gate = jnp.zeros((D_MODEL, 1), F32)
    proj_p, _ = norm_proj(xp, nw, w_in, no_gate, TM_PROMPT, TN_PROJ)
    proj_s, _ = norm_proj(xs, nw, w_in, no_gate, xs.shape[0], TN_PROJ)
    z_col = 3 * len(DIL_GROUPS)
    res_p = [dil_band_attention(proj_p, g, bp, tp) for g in range(len(DIL_GROUPS))]
    yp = dil_out_proj([r[0] for r in res_p], [r[1] for r in res_p], proj_p, z_col, w_out, xp, TM_PROMPT)
    ps3 = proj_s.reshape(bs, N_NEW, -1)
    res_s, new_p, new_s = [], [], []
    for g, (window, _) in enumerate(DIL_GROUPS):
        part = lambda c: ps3[:, :, (3 * g + c) * e:(3 * g + c + 1) * e]
        buf_t = caches[g].transpose(0, 2, 3, 4, 1).reshape(bs, 2, e, window)
        knt = jnp.pad(part(1).transpose(0, 2, 1), ((0, 0), (0, 0), (0, 128 - N_NEW)))
        vnt = jnp.pad(part(2).transpose(0, 2, 1), ((0, 0), (0, 0), (0, 128 - N_NEW)))
        res_s.append(dil_sample_attention(_block_diag_queries(part(0), N_HEAD_B, HD_B), knt, vnt, buf_t, g))

        def kv(pr, b_, t_, keep):
            rows = pr.reshape(b_, t_, -1)[:, t_ - keep:, (3 * g + 1) * e:(3 * g + 3) * e]
            return rows.reshape(b_, keep, 2, N_HEAD_B, HD_B)
        new_p.append(kv(proj_p, bp, tp, min(window, tp)))
        new_s.append(jnp.concatenate([caches[g][:, N_NEW:], kv(proj_s, bs, N_NEW, N_NEW)], axis=1))
    flat = lambda a: a.reshape(bs * N_NEW, e)
    ys = dil_out_proj([flat(r[0]) for r in res_s], [flat(r[1]) for r in res_s], proj_s, z_col, w_out, xs,
                      xs.shape[0])
    return yp, ys, tuple(new_p), tuple(new_s)


def mlstm_layer(xp, xs, nw, w_in, b_i, b_f, conv_w, conv_b, norm_w, w_out, conv_state, c0, n0, m0, bp, tp, bs, fw):
    n_main = 5 * E_C
    w_main, w_gate = w_in[:, :n_main], w_in[:, n_main:]
    proj_p, gates_p = norm_proj(xp, nw, w_main, w_gate, TM_PROMPT, TN_PROJ)
    proj_s, gates_s = norm_proj(xs, nw, w_main, w_gate, xs.shape[0], TN_PROJ)
    zeros = lambda *shape: jnp.zeros(shape, F32)
    hn_p, c_p, n_p, m_p = mlstm_mix(proj_p, gates_p, b_i, b_f, conv_w, conv_b, norm_w,
                                    zeros(bp, CONV_W - 1, 2 * E_C), zeros(bp, N_HEAD_C, DH_C, DH_C),
                                    zeros(bp, N_HEAD_C, DH_C), zeros(bp, N_HEAD_C), bp, tp, MLSTM_LC, MLSTM_LC)
    yp = out_proj(hn_p, 0, proj_p, 4, w_out, xp, fw, TM_PROMPT, False)
    pad_t = lambda a: jnp.pad(a.reshape(bs, N_NEW, -1), ((0, 0), (0, SAMPLE_LC - N_NEW), (0, 0))).reshape(
        bs * SAMPLE_LC, -1)
    hn_s, c_s, n_s, m_s = mlstm_mix(pad_t(proj_s), pad_t(gates_s), b_i, b_f, conv_w, conv_b, norm_w,
                                    conv_state, c0, n0, m0, bs, SAMPLE_LC, SAMPLE_LC, N_NEW)
    hn_s = hn_s.reshape(bs, SAMPLE_LC, E_C)[:, :N_NEW].reshape(bs * N_NEW, E_C)
    ys = out_proj(hn_s, 0, proj_s, 4, w_out, xs, fw, xs.shape[0], False)
    qk_p = proj_p[:, :2 * E_C].reshape(bp, tp, 2 * E_C)
    qk_s = proj_s[:, :2 * E_C].reshape(bs, N_NEW, 2 * E_C)
    conv_p = jnp.concatenate([zeros(bp, CONV_W - 1, 2 * E_C), qk_p], axis=1)[:, tp:]
    conv_s = jnp.concatenate([conv_state, qk_s], axis=1)[:, N_NEW:]
    return yp, ys, (c_p, n_p, m_p, conv_p), (c_s, n_s, m_s, conv_s)


def kernel(x_prompt, x_sample, cache_fox_k, cache_fox_v, cache_fox_logf, cache_dil0_kv, cache_dil1_kv, cache_dil2_kv, state_mlstm_C, state_mlstm_n, state_mlstm_m, state_mlstm_conv, page_table, norm_w, final_norm_w, fox_w_in, fox_b_f, fox_w_out, dil_w_in, dil_w_out, mlstm_w_in, mlstm_b_i, mlstm_b_f, mlstm_conv_w, mlstm_conv_b, mlstm_norm_w, mlstm_w_out):
    bp, tp, d = x_prompt.shape
    bs, ts, _ = x_sample.shape
    assert ts == N_NEW and d == D_MODEL
    depth = norm_w.shape[0]
    dil_caches = (cache_dil0_kv, cache_dil1_kv, cache_dil2_kv)
    n_la, n_pool = cache_fox_k.shape[:2]
    fox_kt = cache_fox_k.transpose(0, 1, 3, 4, 2).reshape(n_la, n_pool, N_HEAD_A * HD_A, PAGE)
    fox_vt = cache_fox_v.transpose(0, 1, 3, 4, 2).reshape(n_la, n_pool, N_HEAD_A * HD_A, PAGE)
    fox_lft = cache_fox_logf.transpose(0, 1, 3, 2)
    xp = x_prompt.reshape(bp * tp, d)
    xs = x_sample.reshape(bs * ts, d)
    fox_p, fox_s, dil_p, dil_s, ml_p, ml_s = [], [], [], [], [], []
    for i in range(depth):
        j, kind = divmod(i, 3)
        final = i == depth - 1
        if kind == 0:
            xp, xs, new_p, new_s = fox_layer(xp, xs, norm_w[i], fox_w_in[j], fox_b_f[j], fox_w_out[j],
                                             fox_kt, fox_vt, fox_lft, j, page_table,
                                             bp, tp, bs, final_norm_w, final)
            fox_p.append(new_p)
            fox_s.append(new_s)
        elif kind == 1:
            assert not final
            xp, xs, new_p, new_s = dil_layer(xp, xs, norm_w[i], dil_w_in[j], dil_w_out[j],
                                             tuple(c[j] for c in dil_caches), bp, tp, bs)
            dil_p.append(new_p)
            dil_s.append(new_s)
        else:
            assert not final
            xp, xs, new_p, new_s = mlstm_layer(xp, xs, norm_w[i], mlstm_w_in[j], mlstm_b_i[j], mlstm_b_f[j],
                                               mlstm_conv_w[j], mlstm_conv_b[j], mlstm_norm_w[j], mlstm_w_out[j],
                                               state_mlstm_conv[j], state_mlstm_C[j], state_mlstm_n[j],
                                               state_mlstm_m[j], bp, tp, bs, final_norm_w)
            ml_p.append(new_p)
            ml_s.append(new_s)
    stk = lambda lst, f: jnp.stack([e[f] for e in lst], axis=0)
    return (xp.reshape(bp, tp, d), xs.reshape(bs, ts, d),
            stk(fox_p, 0), stk(fox_p, 1), stk(fox_p, 2),
            stk(fox_s, 0), stk(fox_s, 1), stk(fox_s, 2),
            stk(dil_p, 0), stk(dil_p, 1), stk(dil_p, 2),
            stk(dil_s, 0), stk(dil_s, 1), stk(dil_s, 2),
            stk(ml_p, 0), stk(ml_p, 1), stk(ml_p, 2), stk(ml_p, 3),
            stk(ml_s, 0), stk(ml_s, 1), stk(ml_s, 2), stk(ml_s, 3))
```

```python
import functools

import numpy as np
import jax
import jax.numpy as jnp
from jax import lax
from jax.experimental import pallas as pl
from jax.experimental.pallas import tpu as pltpu

F32 = jnp.float32
BF16 = jnp.bfloat16
NEG = -1e30
EPS = 1e-6

D_MODEL = 1024
PAGE = 128
N_HEAD_A = 16
HD_A = 64
N_HEAD_B = 16
HD_B = 64
DIL_GROUPS = ((128, 1), (512, 4), (2048, 16))
BAND = 128
N_HEAD_C = 4
E_C = 2 * D_MODEL
DH_C = E_C // N_HEAD_C
CONV_W = 4
N_NEW = 4
ROWS = N_NEW * N_HEAD_A
VMEM_LIMIT = 56 * 1024 * 1024


def _cparams(sem):
    return pltpu.CompilerParams(dimension_semantics=sem, vmem_limit_bytes=VMEM_LIMIT)


def _split3(x):
    hi = x.astype(BF16)
    r = x - hi.astype(F32)
    mid = r.astype(BF16)
    lo = (r - mid.astype(F32)).astype(BF16)
    return hi, mid, lo


def _dot(a, b):
    return jnp.dot(a, b, preferred_element_type=F32)


def _dot_nt(a, b):
    return lax.dot_general(a, b, (((1,), (1,)), ((), ())), preferred_element_type=F32)


def _dot_tn(a, b):
    return lax.dot_general(a, b, (((0,), (0,)), ((), ())), preferred_element_type=F32)


def _dot3_left(x, sel):
    hi, mid, lo = _split3(x)
    return _dot(hi, sel) + _dot(mid, sel) + _dot(lo, sel)


def _dot3_right(sel, x):
    hi, mid, lo = _split3(x)
    return _dot(sel, hi) + _dot(sel, mid) + _dot(sel, lo)


def _log_sigmoid(x):
    return jnp.minimum(x, 0.0) - jnp.log1p(jnp.exp(-jnp.abs(x)))


def _silu(z):
    return z / (1.0 + jnp.exp(-z))


def _sigmoid(z):
    return 1.0 / (1.0 + jnp.exp(-z))


def _head_mask(n_head, hd):
    h_idx = np.arange(n_head * hd) // hd
    return (h_idx[None, :] == np.arange(n_head)[:, None]).astype(np.float32)


def _norm_proj_kernel(x_ref, nw_ref, w_ref, wgh_ref, wgl_ref, o_ref, g_ref, h_ref):
    @pl.when(pl.program_id(1) == 0)
    def _():
        x = x_ref[...]
        h = x * lax.rsqrt(jnp.mean(x * x, axis=-1, keepdims=True) + EPS) * nw_ref[...]
        hb = h.astype(BF16)
        hl = (h - hb.astype(F32)).astype(BF16)
        h_ref[...] = hb
        g_ref[...] = _dot(hb, wgh_ref[...]) + _dot(hb, wgl_ref[...]) + _dot(hl, wgh_ref[...])

    o_ref[...] = _dot(h_ref[...], w_ref[...])


def norm_proj(x, nw, w, wg, tm, tn):
    n, d = x.shape
    nout = w.shape[1]
    wgp = jnp.zeros((d, 128), F32).at[:, : wg.shape[1]].set(wg)
    wgh = wgp.astype(BF16)
    wgl = (wgp - wgh.astype(F32)).astype(BF16)
    return pl.pallas_call(
        _norm_proj_kernel,
        grid=(n // tm, nout // tn),
        in_specs=[
            pl.BlockSpec((tm, d), lambda i, j: (i, 0)),
            pl.BlockSpec((1, d), lambda i, j: (0, 0)),
            pl.BlockSpec((d, tn), lambda i, j: (0, j)),
            pl.BlockSpec((d, 128), lambda i, j: (0, 0)),
            pl.BlockSpec((d, 128), lambda i, j: (0, 0)),
        ],
        out_specs=[
            pl.BlockSpec((tm, tn), lambda i, j: (i, j)),
            pl.BlockSpec((tm, 128), lambda i, j: (i, 0)),
        ],
        out_shape=[jax.ShapeDtypeStruct((n, nout), F32), jax.ShapeDtypeStruct((n, 128), F32)],
        scratch_shapes=[pltpu.VMEM((tm, d), BF16)],
        compiler_params=_cparams(("parallel", "arbitrary")),
        name="norm_proj",
    )(x, nw.reshape(1, d), w.astype(BF16), wgh, wgl)


def _out_proj_kernel(a_ref, z_ref, w_ref, x_ref, fw_ref, o_ref, *, final_norm):
    u = (a_ref[...] * _silu(z_ref[...])).astype(BF16)
    y = x_ref[...] + _dot(u, w_ref[...])
    if final_norm:
        y = y * lax.rsqrt(jnp.mean(y * y, axis=-1, keepdims=True) + EPS) * fw_ref[...]
    o_ref[...] = y


def out_proj(a, a_col, z, z_col, w, x, fw, tm, final_norm):
    n, d = x.shape
    e = w.shape[0]
    return pl.pallas_call(
        functools.partial(_out_proj_kernel, final_norm=final_norm),
        grid=(n // tm,),
        in_specs=[
            pl.BlockSpec((tm, e), lambda i: (i, a_col)),
            pl.BlockSpec((tm, e), lambda i: (i, z_col)),
            pl.BlockSpec((e, d), lambda i: (0, 0)),
            pl.BlockSpec((tm, d), lambda i: (i, 0)),
            pl.BlockSpec((1, d), lambda i: (0, 0)),
        ],
        out_specs=pl.BlockSpec((tm, d), lambda i: (i, 0)),
        out_shape=jax.ShapeDtypeStruct((n, d), F32),
        compiler_params=_cparams(("parallel",)),
        name="out_proj",
    )(a, z, w.astype(BF16), x, fw.reshape(1, d))


def _dil_out_proj_kernel(o0_ref, o1_ref, o2_ref, l0_ref, l1_ref, l2_ref, z_ref, w_ref, x_ref, o_ref):
    l0, l1, l2 = l0_ref[...], l1_ref[...], l2_ref[...]
    m = jnp.maximum(jnp.maximum(l0, l1), l2)
    e0, e1, e2 = jnp.exp(l0 - m), jnp.exp(l1 - m), jnp.exp(l2 - m)
    o = (e0 * o0_ref[...] + e1 * o1_ref[...] + e2 * o2_ref[...]) / (e0 + e1 + e2)
    u = (o * _silu(z_ref[...])).astype(BF16)
    o_ref[...] = x_ref[...] + _dot(u, w_ref[...])


def dil_out_proj(os_, ls_, z, z_col, w, x, tm):
    n, d = x.shape
    e = w.shape[0]
    blk = pl.BlockSpec((tm, e), lambda i: (i, 0))
    return pl.pallas_call(
        _dil_out_proj_kernel,
        grid=(n // tm,),
        in_specs=[blk] * 6 + [
            pl.BlockSpec((tm, e), lambda i: (i, z_col)),
            pl.BlockSpec((e, d), lambda i: (0, 0)),
            pl.BlockSpec((tm, d), lambda i: (i, 0)),
        ],
        out_specs=pl.BlockSpec((tm, d), lambda i: (i, 0)),
        out_shape=jax.ShapeDtypeStruct((n, d), F32),
        compiler_params=_cparams(("parallel",)),
        name="dil_out_proj",
    )(*os_, *ls_, z, w.astype(BF16), x)


def _fox_proj_kernel(x_ref, nw_ref, w_ref, wgh_ref, wgl_ref, q_ref, kt_ref, vt_ref, z_ref, g_ref, h_ref):
    j = pl.program_id(1)

    @pl.when(j == 0)
    def _():
        x = x_ref[...]
        h = x * lax.rsqrt(jnp.mean(x * x, axis=-1, keepdims=True) + EPS) * nw_ref[...]
        hb = h.astype(BF16)
        hl = (h - hb.astype(F32)).astype(BF16)
        h_ref[...] = hb
        g_ref[...] = _dot_nt(hb, wgh_ref[...]) + _dot_nt(hb, wgl_ref[...]) + _dot_nt(hl, wgh_ref[...])
        q_ref[...] = _dot_nt(hb, w_ref[0])

    @pl.when(j == 1)
    def _():
        kt_ref[0] = _dot_nt(w_ref[0], h_ref[...])

    @pl.when(j == 2)
    def _():
        vt_ref[0] = _dot_nt(w_ref[0], h_ref[...])

    @pl.when(j == 3)
    def _():
        z_ref[...] = _dot_nt(h_ref[...], w_ref[0])


def fox_proj(x, nw, w_in, bsz, t, tm):
    n, d = x.shape
    e = N_HEAD_A * HD_A
    nblk = t // tm
    wt = w_in.T
    w4 = jnp.stack([wt[:e], wt[e:2 * e], wt[2 * e:3 * e], wt[3 * e + N_HEAD_A:]], axis=0).astype(BF16)
    wg = jnp.zeros((128, d), F32).at[:N_HEAD_A].set(wt[3 * e:3 * e + N_HEAD_A])
    wgh = wg.astype(BF16)
    wgl = (wg - wgh.astype(F32)).astype(BF16)
    row_blk = pl.BlockSpec((tm, e), lambda i, j: (i, 0))
    t_blk = pl.BlockSpec((1, e, tm), lambda i, j: (i // nblk, 0, i % nblk))
    return pl.pallas_call(
        _fox_proj_kernel,
        grid=(n // tm, 4),
        in_specs=[
            pl.BlockSpec((tm, d), lambda i, j: (i, 0)),
            pl.BlockSpec((1, d), lambda i, j: (0, 0)),
            pl.BlockSpec((1, e, d), lambda i, j: (j, 0, 0)),
            pl.BlockSpec((128, d), lambda i, j: (0, 0)),
            pl.BlockSpec((128, d), lambda i, j: (0, 0)),
        ],
        out_specs=[row_blk, t_blk, t_blk, row_blk, pl.BlockSpec((tm, 128), lambda i, j: (i, 0))],
        out_shape=[jax.ShapeDtypeStruct((n, e), F32), jax.ShapeDtypeStruct((bsz, e, t), F32),
                   jax.ShapeDtypeStruct((bsz, e, t), F32), jax.ShapeDtypeStruct((n, e), F32),
                   jax.ShapeDtypeStruct((n, 128), F32)],
        scratch_shapes=[pltpu.VMEM((tm, d), BF16)],
        compiler_params=_cparams(("parallel", "arbitrary")),
        name="fox_proj",
    )(x, nw.reshape(1, d), w4, wgh, wgl)


AUG_Q_C = (0, 1, 2)
AUG_K_C = (3, 4, 5)
AUG_PER_HEAD = 6
ONES_LANE = N_HEAD_A

def _fox_sel_matrices():
    selq = np.zeros((3, 128, D_MODEL), np.float32)
    selk = np.zeros((3, 128, D_MODEL), np.float32)
    for hp in range(N_HEAD_A // 2):
        for hh in range(2):
            base = hp * 128 + hh * AUG_PER_HEAD
            for p in range(3):
                selq[p, 2 * hp + hh, base + AUG_Q_C[p]] = 1.0
                selk[p, 2 * hp + hh, base + AUG_K_C[p]] = -1.0
                selq[0, ONES_LANE, base + AUG_K_C[p]] = 1.0
                selk[0, ONES_LANE, base + AUG_Q_C[p]] = 1.0
    return selq, np.ascontiguousarray(selk.transpose(0, 2, 1))


def _fox_prep_kernel(q_ref, kt_ref, vt_ref, g_ref, bf_ref, selq_ref, selkt_ref,
                     qb_ref, qa_ref, kbt_ref, kat_ref, vbt_ref, lft_ref, carry_ref, *, tb):
    @pl.when(pl.program_id(1) == 0)
    def _():
        carry_ref[...] = jnp.zeros_like(carry_ref)

    lf = _log_sigmoid(g_ref[...] + bf_ref[...])
    lft_ref[0] = lf.T[:N_HEAD_A]
    row = lax.broadcasted_iota(jnp.int32, (tb, tb), 0)
    col = lax.broadcasted_iota(jnp.int32, (tb, tb), 1)
    tril = (col <= row).astype(BF16)
    c = _dot3_right(tril, lf) + carry_ref[...]
    carry_ref[...] = c[tb - 1:tb, :]
    lane = lax.broadcasted_iota(jnp.int32, c.shape, 1)
    c1 = jnp.where(lane == ONES_LANE, 1.0, c)
    parts = _split3(c1)
    parts_t = _split3(c1.T)
    qa = _dot(parts[0], selq_ref[0])
    kat = _dot(selkt_ref[0], parts_t[0])
    for p in range(1, 3):
        qa = qa + _dot(parts[p], selq_ref[p])
        kat = kat + _dot(selkt_ref[p], parts_t[p])
    qa_ref[...] = qa.astype(BF16)
    kat_ref[0] = kat.astype(BF16)
    qb_ref[...] = (q_ref[...] * (HD_A ** -0.5)).astype(BF16)
    kbt_ref[0] = kt_ref[0].astype(BF16)
    vbt_ref[0] = vt_ref[0].astype(BF16)


def fox_prep(q, kt, vt, gates, b_f, bsz, t, tb):
    n = bsz * t
    nb = t // tb
    e = N_HEAD_A * HD_A
    selq, selkt = _fox_sel_matrices()
    bfp = jnp.zeros((1, 128), F32).at[0, :N_HEAD_A].set(b_f)
    row_blk = pl.BlockSpec((tb, e), lambda b, i: (b * nb + i, 0))
    t_blk = pl.BlockSpec((1, e, tb), lambda b, i: (b, 0, i))
    g_blk = pl.BlockSpec((tb, 128), lambda b, i: (b * nb + i, 0))
    return pl.pallas_call(
        functools.partial(_fox_prep_kernel, tb=tb),
        grid=(bsz, nb),
        in_specs=[row_blk, t_blk, t_blk, g_blk, pl.BlockSpec((1, 128), lambda b, i: (0, 0)),
                  pl.BlockSpec((3, 128, e), lambda b, i: (0, 0, 0)),
                  pl.BlockSpec((3, e, 128), lambda b, i: (0, 0, 0))],
        out_specs=[row_blk, row_blk, t_blk, t_blk, t_blk,
                   pl.BlockSpec((1, N_HEAD_A, tb), lambda b, i: (b, 0, i))],
        out_shape=[jax.ShapeDtypeStruct((n, e), BF16)] * 2 + [jax.ShapeDtypeStruct((bsz, e, t), BF16)] * 3
        + [jax.ShapeDtypeStruct((bsz, N_HEAD_A, t), F32)],
        scratch_shapes=[pltpu.VMEM((1, 128), F32)],
        compiler_params=_cparams(("parallel", "arbitrary")),
        name="fox_prep",
    )(q, kt, vt, gates, bfp, jnp.asarray(selq, BF16), jnp.asarray(selkt, BF16))


def _fox_attn_kernel(qi_ref, kj_ref, qb_ref, qa_ref, kbt_ref, kat_ref, vbt_ref, o_ref,
                     lhs_ref, m_ref, acc_ref, *, tq):
    s_idx = pl.program_id(2)
    qi = qi_ref[s_idx]
    kj = kj_ref[s_idx]

    @pl.when(kj == 0)
    def _init():
        qcat = jnp.concatenate([qb_ref[...], qa_ref[...]], axis=1)
        lane = lax.broadcasted_iota(jnp.int32, (1, 256), 1)
        for hh in range(2):
            lo = 128 + hh * AUG_PER_HEAD
            keep = ((lane >= hh * 64) & (lane < hh * 64 + 64)) | ((lane >= lo) & (lane < lo + AUG_PER_HEAD))
            lhs_ref[hh] = jnp.where(keep, qcat, jnp.zeros_like(qcat))
        m_ref[...] = jnp.full_like(m_ref, NEG)
        acc_ref[...] = jnp.zeros_like(acc_ref)

    def step(diagonal):
        kcat = jnp.concatenate([kbt_ref[0], kat_ref[0]], axis=0)
        v = vbt_ref[0]
        vrow = lax.broadcasted_iota(jnp.int32, v.shape, 0)
        for hh in range(2):
            vh = jnp.where((vrow >= hh * 64) & (vrow < hh * 64 + 64), v, jnp.ones_like(v))
            s = _dot(lhs_ref[hh], kcat)
            if diagonal:
                row = lax.broadcasted_iota(jnp.int32, s.shape, 0)
                col = lax.broadcasted_iota(jnp.int32, s.shape, 1)
                s = jnp.where(col <= row, s, NEG)
            m_prev = m_ref[hh]
            m_new = jnp.maximum(m_prev, jnp.max(s, axis=1, keepdims=True))
            p = jnp.exp(s - m_new)
            acc_ref[hh] = jnp.exp(m_prev - m_new) * acc_ref[hh] + _dot_nt(p.astype(BF16), vh)
            m_ref[hh] = m_new

    @pl.when(kj < qi)
    def _off():
        step(False)

    @pl.when(kj == qi)
    def _diag():
        step(True)
        lane = lax.broadcasted_iota(jnp.int32, (tq, 128), 1)
        a0, a1 = acc_ref[0], acc_ref[1]
        o_ref[...] = jnp.where(lane < 64, a0 / pltpu.roll(a0, 64, 1), a1 / pltpu.roll(a1, 64, 1))


def fox_attention_prompt(qb, qa, kbt, kat, vbt, bsz, t, tq):
    n = bsz * t
    nq = t // tq
    qi = np.concatenate([np.full(i + 1, i, np.int32) for i in range(nq)])
    kj = np.concatenate([np.arange(i + 1, dtype=np.int32) for i in range(nq)])
    q_blk = pl.BlockSpec((tq, 128), lambda b, hp, s, qi_r, kj_r: (b * nq + qi_r[s], hp))
    k_blk = pl.BlockSpec((1, 128, tq), lambda b, hp, s, qi_r, kj_r: (b, hp, kj_r[s]))
    return pl.pallas_call(
        functools.partial(_fox_attn_kernel, tq=tq),
        grid_spec=pltpu.PrefetchScalarGridSpec(
            num_scalar_prefetch=2,
            grid=(bsz, N_HEAD_A // 2, len(qi)),
            in_specs=[q_blk, q_blk, k_blk, k_blk, k_blk],
            out_specs=q_blk,
            scratch_shapes=[pltpu.VMEM((2, tq, 256), BF16), pltpu.VMEM((2, tq, 1), F32),
                            pltpu.VMEM((2, tq, 128), F32)],
        ),
        out_shape=jax.ShapeDtypeStruct((n, D_MODEL), F32),
        compiler_params=_cparams(("parallel", "parallel", "arbitrary")),
        name="fox_attn",
    )(jnp.asarray(qi), jnp.asarray(kj), qb, qa, kbt, kat, vbt)


def _flash_update(q_ref, kt, vt, bias, valid, m_ref, l_ref, acc_ref):
    s = _dot(q_ref[0], kt.astype(BF16)) + bias
    if valid is not None:
        s = jnp.where(valid, s, NEG)
    m_prev = m_ref[...]
    m_new = jnp.maximum(m_prev, jnp.max(s, axis=1, keepdims=True))
    alpha = jnp.exp(m_prev - m_new)
    pr = jnp.exp(s - m_new)
    l_ref[...] = alpha * l_ref[...] + jnp.sum(pr, axis=1, keepdims=True)
    acc_ref[...] = alpha * acc_ref[...] + _dot_nt(pr.astype(BF16), vt.astype(BF16))
    m_ref[...] = m_new


def _head_diag(x, hmask):
    n_head = hmask.shape[0]
    return jnp.sum(x.reshape(N_NEW, n_head, x.shape[1]) * hmask[None], axis=1)


def _fox_sample_kernel(pt_ref, q_ref, *refs, n_groups, gp):
    k_refs, v_refs, lf_refs = refs[:gp], refs[gp:2 * gp], refs[2 * gp:3 * gp]
    (kn_ref, vn_ref, gn_ref, bf_ref, hmask_ref, o_ref, lfn_ref,
     m_ref, l_ref, acc_ref, carry_ref, ecol_ref) = refs[3 * gp:]
    p = pl.program_id(1)
    w = gp * PAGE

    @pl.when(p == 0)
    def _new_tokens():
        m_ref[...] = jnp.full_like(m_ref, NEG)
        l_ref[...] = jnp.zeros_like(l_ref)
        acc_ref[...] = jnp.zeros_like(acc_ref)
        carry_ref[...] = jnp.zeros_like(carry_ref)
        lfn = _log_sigmoid(gn_ref[0] + bf_ref[...])
        lfn_ref[0] = lfn
        r_i = lax.broadcasted_iota(jnp.int32, (128, 128), 0)
        c_i = lax.broadcasted_iota(jnp.int32, (128, 128), 1)
        incl = ((r_i <= c_i) & (r_i < N_NEW)).astype(BF16)
        e = _dot3_left(lfn, incl)
        e4 = jnp.concatenate([e] * N_NEW, axis=0)
        tok = lax.broadcasted_iota(jnp.int32, (ROWS, 128), 0) // N_HEAD_A
        lane = lax.broadcasted_iota(jnp.int32, (ROWS, 128), 1)
        ecol = jnp.sum(jnp.where(lane == tok, e4, 0.0), axis=1, keepdims=True)
        ecol_ref[...] = ecol
        valid = (lane <= tok) & (lane < N_NEW)
        _flash_update(q_ref, kn_ref[0], vn_ref[0], ecol - e4, valid, m_ref, l_ref, acc_ref)

    @pl.when(p > 0)
    def _pages():
        cat = lambda rs: jnp.concatenate([r[0, 0] for r in rs], axis=1) if gp > 1 else rs[0][0, 0]
        lf = cat(lf_refs)
        r_i = lax.broadcasted_iota(jnp.int32, (w, w), 0)
        c_i = lax.broadcasted_iota(jnp.int32, (w, w), 1)
        after = (r_i > c_i).astype(BF16)
        d = _dot3_left(lf, after) + carry_ref[...]
        carry_ref[...] = carry_ref[...] + jnp.sum(lf, axis=1, keepdims=True)
        d4 = jnp.concatenate([d] * N_NEW, axis=0)
        _flash_update(q_ref, cat(k_refs), cat(v_refs), ecol_ref[...] + d4, None, m_ref, l_ref, acc_ref)

    @pl.when(p == n_groups)
    def _finish():
        o_ref[0] = _head_diag(acc_ref[...] / l_ref[...], hmask_ref[...])


FOX_PAGES_PER_STEP = 4


def fox_attention_sample(qbd, knt, vnt, gnt, b_f, cache_kt, cache_vt, cache_lft, layer, page_table):
    bs, n_pages = page_table.shape
    gp = FOX_PAGES_PER_STEP
    assert n_pages % gp == 0
    n_groups = n_pages // gp
    e = N_HEAD_A * HD_A

    def page_map(c):
        def index_map(b, p, pt):
            grp = n_groups - jnp.maximum(p, 1)
            return (layer, pt[b * n_pages + grp * gp + c], 0, 0)
        return index_map

    per_b = lambda shape: pl.BlockSpec(shape, lambda b, p, pt: (b, 0, 0))
    const2 = lambda shape: pl.BlockSpec(shape, lambda b, p, pt: (0, 0))
    kv_specs = [pl.BlockSpec((1, 1, e, PAGE), page_map(c)) for c in range(gp)]
    lf_specs = [pl.BlockSpec((1, 1, N_HEAD_A, PAGE), page_map(c)) for c in range(gp)]
    return pl.pallas_call(
        functools.partial(_fox_sample_kernel, n_groups=n_groups, gp=gp),
        grid_spec=pltpu.PrefetchScalarGridSpec(
            num_scalar_prefetch=1,
            grid=(bs, n_groups + 1),
            in_specs=[per_b((1, ROWS, e))] + kv_specs + kv_specs + lf_specs + [
                per_b((1, e, 128)), per_b((1, e, 128)), per_b((1, N_HEAD_A, 128)),
                const2((N_HEAD_A, 1)), const2((N_HEAD_A, e))],
            out_specs=[per_b((1, N_NEW, e)), per_b((1, N_HEAD_A, 128))],
            scratch_shapes=[pltpu.VMEM((ROWS, 1), F32), pltpu.VMEM((ROWS, 1), F32),
                            pltpu.VMEM((ROWS, e), F32), pltpu.VMEM((N_HEAD_A, 1), F32),
                            pltpu.VMEM((ROWS, 1), F32)],
        ),
        out_shape=[jax.ShapeDtypeStruct((bs, N_NEW, e), F32),
                   jax.ShapeDtypeStruct((bs, N_HEAD_A, 128), F32)],
        compiler_params=_cparams(("parallel", "arbitrary")),
        name="fox_sample",
    )(page_table.reshape(-1), qbd, *([cache_kt] * gp), *([cache_vt] * gp), *([cache_lft] * gp),
      knt, vnt, gnt, b_f.reshape(N_HEAD_A, 1), jnp.asarray(_head_mask(N_HEAD_A, HD_A)))


def _alibi_slope(h):
    return float(2.0 ** (-8.0 * (h + 1.0) / N_HEAD_B))


def _shift_in(x, nxt):
    cw = x.shape[1]
    y = pltpu.roll(x, cw - N_NEW, 1)
    tail = pltpu.roll(nxt, 128 - N_NEW, 1)
    lane = lax.broadcasted_iota(jnp.int32, (x.shape[0], 128), 1)
    last = jnp.where(lane < 128 - N_NEW, y[:, cw - 128:], tail)
    return last if cw == 128 else jnp.concatenate([y[:, :cw - 128], last], axis=1)


def _dil_sample_kernel(q_ref, k_ref, v_ref, kn_ref, vn_ref, slope_ref, hmask_ref, o_ref, lse_ref, buf_ref,
                       m_ref, l_ref, acc_ref, nxt_ref, *, dil, window, cw, n_chunks):
    p = pl.program_id(1)
    tok = lax.broadcasted_iota(jnp.int32, (ROWS, cw), 0) // N_HEAD_B
    lane = lax.broadcasted_iota(jnp.int32, (ROWS, cw), 1)
    slope = slope_ref[...]

    @pl.when(p == 0)
    def _new_tokens():
        m_ref[...] = jnp.full_like(m_ref, NEG)
        l_ref[...] = jnp.zeros_like(l_ref)
        acc_ref[...] = jnp.zeros_like(acc_ref)
        nxt_ref[0] = kn_ref[0]
        nxt_ref[1] = vn_ref[0]
        tok_n = lax.broadcasted_iota(jnp.int32, (ROWS, 128), 0) // N_HEAD_B
        lane_n = lax.broadcasted_iota(jnp.int32, (ROWS, 128), 1)
        dist = tok_n - lane_n
        valid = (dist >= 0) & ((dist & (dil - 1)) == 0) & (lane_n < N_NEW)
        _flash_update(q_ref, kn_ref[0], vn_ref[0], -slope * dist.astype(F32), valid, m_ref, l_ref, acc_ref)

    @pl.when(p > 0)
    def _chunk():
        k, v = k_ref[0, 0], v_ref[0, 0]
        dist = window + tok - ((n_chunks - p) * cw + lane)
        valid = ((dist & (dil - 1)) == 0) & (dist <= window)
        _flash_update(q_ref, k, v, -slope * dist.astype(F32), valid, m_ref, l_ref, acc_ref)
        buf_ref[0, 0] = _shift_in(k, nxt_ref[0])
        buf_ref[0, 1] = _shift_in(v, nxt_ref[1])
        nxt_ref[0] = k[:, :128]
        nxt_ref[1] = v[:, :128]

    @pl.when(p == n_chunks)
    def _finish():
        hmask = hmask_ref[...]
        l = l_ref[...]
        o_ref[0] = _head_diag(acc_ref[...] / l, hmask)
        lse = m_ref[...] + jnp.log(l)
        lse_ref[0] = _head_diag(jnp.broadcast_to(lse, acc_ref.shape), hmask)


def dil_sample_attention(qbd, knt, vnt, buf_t, g):
    window, dil = DIL_GROUPS[g]
    assert dil & (dil - 1) == 0 and buf_t.shape[3] == window
    bs = qbd.shape[0]
    e = N_HEAD_B * HD_B
    cw = min(window, 512)
    n_chunks = window // cw
    slope = np.array([[_alibi_slope(r % N_HEAD_B)] for r in range(ROWS)], np.float32)
    per_b = lambda shape: pl.BlockSpec(shape, lambda b, p: (b, 0, 0))
    newest_first = lambda p: n_chunks - jnp.maximum(p, 1)
    chunk = lambda kv: pl.BlockSpec((1, 1, e, cw), lambda b, p, kv=kv: (b, kv, 0, newest_first(p)))
    return pl.pallas_call(
        functools.partial(_dil_sample_kernel, dil=dil, window=window, cw=cw, n_chunks=n_chunks),
        grid=(bs, n_chunks + 1),
        in_specs=[per_b((1, ROWS, e)), chunk(0), chunk(1), per_b((1, e, 128)), per_b((1, e, 128)),
                  pl.BlockSpec((ROWS, 1), lambda b, p: (0, 0)),
                  pl.BlockSpec((N_HEAD_B, e), lambda b, p: (0, 0))],
        out_specs=[per_b((1, N_NEW, e))] * 2 + [pl.BlockSpec((1, 2, e, cw), lambda b, p: (b, 0, 0, newest_first(p)))],
        out_shape=[jax.ShapeDtypeStruct((bs, N_NEW, e), F32)] * 2 + [jax.ShapeDtypeStruct(buf_t.shape, F32)],
        scratch_shapes=[pltpu.VMEM((ROWS, 1), F32), pltpu.VMEM((ROWS, 1), F32), pltpu.VMEM((ROWS, e), F32),
                        pltpu.VMEM((2, e, 128), F32)],
        compiler_params=_cparams(("parallel", "arbitrary")),
        name="dil_sample%d" % g,
    )(qbd, buf_t, buf_t, knt, vnt, jnp.asarray(slope), jnp.asarray(_head_mask(N_HEAD_B, HD_B)))


def _block_diag_queries(q, n_head, hd):
    bs = q.shape[0]
    qbd = (q * (hd ** -0.5))[:, :, None, :] * jnp.asarray(_head_mask(n_head, hd))[None, None]
    return qbd.reshape(bs, N_NEW * n_head, n_head * hd).astype(BF16)


def _new_token_major(x_t, bs):
    f = x_t.shape[0]
    return jnp.pad(x_t.reshape(f, bs, N_NEW).transpose(1, 0, 2), ((0, 0), (0, 0), (0, 128 - N_NEW)))


DIL_CHUNK_TOKENS = 2048
DIL_UNROLL = 4


def _dil_band_kernel(q_ref, kp_ref, kc_ref, vp_ref, vc_ref, slope_ref, o_ref, lse_ref, *, dil, nblk):
    n = pl.program_id(1)
    r = BAND
    span = r * dil
    qrow = lax.broadcasted_iota(jnp.int32, (r, 2 * r), 0)
    kcol = lax.broadcasted_iota(jnp.int32, (r, 2 * r), 1)
    dist = r + qrow - kcol
    in_band = (dist >= 0) & (dist <= r)
    distf = (dist * dil).astype(F32)
    lane = lax.broadcasted_iota(jnp.int32, (r, 128), 1)
    first = lane < HD_B

    def unit(blk, res):
        def rows(b_):
            return pl.ds(b_ * span + res, r, stride=dil) if dil > 1 else pl.ds(b_ * span, r)
        if blk == 0:
            k_prev, v_prev = kp_ref[rows(0), :], vp_ref[rows(0), :]
            valid = in_band & ((kcol >= r) | (n > 0))
        else:
            k_prev, v_prev = kc_ref[rows(blk - 1), :], vc_ref[rows(blk - 1), :]
            valid = in_band
        q = q_ref[rows(blk), :] * (HD_B ** -0.5)
        k2 = jnp.concatenate([k_prev, kc_ref[rows(blk), :]], axis=0).astype(BF16)
        v2 = jnp.concatenate([v_prev, vc_ref[rows(blk), :]], axis=0).astype(BF16)
        outs, lses = [], []
        for hh in range(2):
            qh = jnp.where(first == (hh == 0), q, 0.0).astype(BF16)
            s = _dot_nt(qh, k2) - slope_ref[0, hh:hh + 1, 0:1] * distf
            s = jnp.where(valid, s, NEG)
            m = jnp.max(s, axis=1, keepdims=True)
            p = jnp.exp(s - m)
            l = jnp.sum(p, axis=1, keepdims=True)
            outs.append(_dot(p.astype(BF16), v2) / l)
            lses.append(m + jnp.log(l))
        o_ref[rows(blk), :] = jnp.where(first, outs[0], outs[1])
        lse_ref[rows(blk), :] = jnp.where(first, lses[0], lses[1])

    for blk in range(nblk):
        if dil == 1:
            unit(blk, 0)
        else:
            def body(res, carry, blk=blk):
                unit(blk, res)
                return carry
            lax.fori_loop(0, dil, body, 0, unroll=min(dil, DIL_UNROLL))


def dil_band_attention(proj, g, bsz, t):
    window, dil = DIL_GROUPS[g]
    assert window // dil == BAND
    span = BAND * dil
    nblk = max(DIL_CHUNK_TOKENS // span, 1)
    ch = nblk * span
    nch = t // ch
    npair = N_HEAD_B // 2
    slopes = np.zeros((npair, 8, 128), np.float32)
    for h in range(N_HEAD_B):
        slopes[h // 2, h % 2, :] = _alibi_slope(h)

    def cur(c):
        return pl.BlockSpec((ch, 128), lambda b, n, hp, c=c: (b * nch + n, c * npair + hp))

    def prev(c):
        return pl.BlockSpec((span, 128),
                            lambda b, n, hp, c=c: (jnp.maximum((b * nch + n) * nblk - 1, 0), c * npair + hp))

    out_blk = pl.BlockSpec((ch, 128), lambda b, n, hp: (b * nch + n, hp))
    return pl.pallas_call(
        functools.partial(_dil_band_kernel, dil=dil, nblk=nblk),
        grid=(bsz, nch, npair),
        in_specs=[cur(3 * g), prev(3 * g + 1), cur(3 * g + 1), prev(3 * g + 2), cur(3 * g + 2),
                  pl.BlockSpec((1, 8, 128), lambda b, n, hp: (hp, 0, 0))],
        out_specs=[out_blk, out_blk],
        out_shape=[jax.ShapeDtypeStruct((bsz * t, D_MODEL), F32)] * 2,
        compiler_params=_cparams(("parallel", "arbitrary", "arbitrary")),
        name="dil_band%d" % g,
    )(proj, proj, proj, proj, proj, jnp.asarray(slopes))


def _mlstm_kernel(qp_ref, kp_ref, v_ref, op_ref, g_ref, gb_ref, cwq_ref, cwk_ref, cbq_ref, cbk_ref,
                  nw_ref, csq_ref, csk_ref, c0_ref, n0_ref, m0_ref,
                  hn_ref, c_out_ref, n_out_ref, m_out_ref,
                  c_s, n_s, m_s, xq_s, xk_s, *, lc, n_valid, nc):
    h = pl.program_id(1)
    c = pl.program_id(2)

    @pl.when(c == 0)
    def _():
        c_s[...] = c0_ref[0, 0]
        n_s[...] = n0_ref[0, 0]
        m_s[...] = m0_ref[0, 0]
        xq_s[0:8, :] = csq_ref[0]
        xk_s[0:8, :] = csk_ref[0]

    def conv(x_ref, xs, cw_ref, cb_ref):
        xs[8:8 + lc, :] = x_ref[...]
        acc = cb_ref[...] + cw_ref[0:1, :] * xs[pl.ds(8 - (CONV_W - 1), lc), :]
        for w in range(1, CONV_W):
            acc = acc + cw_ref[w:w + 1, :] * xs[pl.ds(8 - (CONV_W - 1) + w, lc), :]
        xs[0:8, :] = xs[lc:lc + 8, :]
        return _silu(acc)

    q = conv(qp_ref, xq_s, cwq_ref, cbq_ref)
    k = conv(kp_ref, xk_s, cwk_ref, cbk_ref) * (DH_C ** -0.5)
    v = v_ref[...]

    g = g_ref[...] + gb_ref[...]
    lane = lax.broadcasted_iota(jnp.int32, g.shape, 1)
    li_col = jnp.sum(jnp.where(lane == h, g, 0.0), axis=1, keepdims=True)
    lf_col = _log_sigmoid(jnp.sum(jnp.where(lane == N_HEAD_C + h, g, 0.0), axis=1, keepdims=True))
    if n_valid < lc:
        rows = lax.broadcasted_iota(jnp.int32, (lc, 1), 0)
        li_col = jnp.where(rows < n_valid, li_col, NEG)
        lf_col = jnp.where(rows < n_valid, lf_col, 0.0)
    row = lax.broadcasted_iota(jnp.int32, (lc, lc), 0)
    col = lax.broadcasted_iota(jnp.int32, (lc, lc), 1)
    eye = row == col
    tril = col <= row
    to_row = lambda x_col: jnp.sum(jnp.where(eye, x_col, 0.0), axis=0, keepdims=True)
    lf_row = to_row(lf_col)
    li_row = to_row(li_col)
    b_col = jnp.sum(jnp.where(tril, lf_row, 0.0), axis=1, keepdims=True)
    b_row = to_row(b_col)
    m_prev = m_s[...]
    d = jnp.where(tril, b_col - b_row + li_row, NEG)
    m_t = jnp.maximum(b_col + m_prev, jnp.max(d, axis=1, keepdims=True))
    dexp = jnp.where(tril, jnp.exp(d - m_t), 0.0)
    inter = jnp.exp(b_col + m_prev - m_t)
    qb, kb, vb = q.astype(BF16), k.astype(BF16), v.astype(BF16)
    w = _dot_nt(qb, kb) * dexp
    num = _dot(w.astype(BF16), vb) + inter * _dot(qb, c_s[...].astype(BF16))
    den = jnp.sum(w, axis=1, keepdims=True) + inter * jnp.sum(q * n_s[...], axis=1, keepdims=True)
    hh = num / jnp.maximum(jnp.abs(den), jnp.exp(-m_t))
    m_new = m_t[lc - 1:lc, :]
    b_last = b_col[lc - 1:lc, :]
    g_col = jnp.exp(b_last - b_col + li_col - m_new)
    decay = jnp.exp(b_last + m_prev - m_new)
    kg = k * g_col
    c_s[...] = decay * c_s[...] + _dot_tn(kg.astype(BF16), vb)
    n_s[...] = decay * n_s[...] + jnp.sum(kg, axis=0, keepdims=True)
    m_s[...] = m_new

    ho = hh * _sigmoid(op_ref[...])
    mu = jnp.mean(ho, axis=1, keepdims=True)
    var = jnp.mean(jnp.square(ho - mu), axis=1, keepdims=True)
    hn_ref[...] = (ho - mu) * lax.rsqrt(var + EPS) * nw_ref[...]

    @pl.when(c == nc - 1)
    def _():
        c_out_ref[0, 0] = c_s[...]
        n_out_ref[0, 0] = n_s[...]
        m_out_ref[0, 0] = m_s[...]


def mlstm_mix(proj, gates, b_i, b_f, conv_w, conv_b, norm_w, conv_state, c0, n0, m0, bsz, t, lc, n_valid):
    n = bsz * t
    nc = t // lc
    assert n_valid == lc or nc == 1
    nh = N_HEAD_C
    gb = jnp.zeros((1, 128), F32).at[0, :nh].set(b_i).at[0, nh:2 * nh].set(b_f)
    cs = jnp.pad(conv_state, ((0, 0), (8 - (CONV_W - 1), 0), (0, 0)))
    blk = lambda c0_: pl.BlockSpec((lc, DH_C), lambda b, h, c, c0_=c0_: (b * nc + c, c0_ + h))
    per_h = lambda rows, c0_: pl.BlockSpec((rows, DH_C), lambda b, h, c, c0_=c0_: (0, c0_ + h))
    st4 = lambda shape: pl.BlockSpec(shape, lambda b, h, c: (b, h, 0, 0))
    hn, c_out, n_out, m_out = pl.pallas_call(
        functools.partial(_mlstm_kernel, lc=lc, n_valid=n_valid, nc=nc),
        grid=(bsz, nh, nc),
        in_specs=[
            blk(0), blk(nh), blk(2 * nh), blk(3 * nh),
            pl.BlockSpec((lc, 128), lambda b, h, c: (b * nc + c, 0)),
            pl.BlockSpec((1, 128), lambda b, h, c: (0, 0)),
            per_h(CONV_W, 0), per_h(CONV_W, nh), per_h(1, 0), per_h(1, nh),
            per_h(1, 0),
            pl.BlockSpec((1, 8, DH_C), lambda b, h, c: (b, 0, h)),
            pl.BlockSpec((1, 8, DH_C), lambda b, h, c: (b, 0, nh + h)),
            st4((1, 1, DH_C, DH_C)), st4((1, 1, 1, DH_C)), st4((1, 1, 1, 1)),
        ],
        out_specs=[
            pl.BlockSpec((lc, DH_C), lambda b, h, c: (b * nc + c, h)),
            st4((1, 1, DH_C, DH_C)), st4((1, 1, 1, DH_C)), st4((1, 1, 1, 1)),
        ],
        out_shape=[
            jax.ShapeDtypeStruct((n, E_C), F32),
            jax.ShapeDtypeStruct((bsz, nh, DH_C, DH_C), F32),
            jax.ShapeDtypeStruct((bsz, nh, 1, DH_C), F32),
            jax.ShapeDtypeStruct((bsz, nh, 1, 1), F32),
        ],
        scratch_shapes=[pltpu.VMEM((DH_C, DH_C), F32), pltpu.VMEM((1, DH_C), F32), pltpu.VMEM((1, 1), F32),
                        pltpu.VMEM((lc + 8, DH_C), F32), pltpu.VMEM((lc + 8, DH_C), F32)],
        compiler_params=_cparams(("parallel", "parallel", "arbitrary")),
        name="mlstm",
    )(proj, proj, proj, proj, gates, gb, conv_w, conv_w, conv_b.reshape(1, -1), conv_b.reshape(1, -1),
      norm_w.reshape(1, -1), cs, cs, c0, n0.reshape(bsz, nh, 1, DH_C), m0.reshape(bsz, nh, 1, 1))
    return hn, c_out, n_out.reshape(bsz, nh, DH_C), m_out.reshape(bsz, nh)


TM_PROMPT = 512
TM_PROJ = 1024
TN_PROJ = 2048
FOX_TQ = 1024
FOX_PREP_TB = 512
MLSTM_LC = 256
SAMPLE_LC = 16


def fox_layer(xp, xs, nw, w_in, b_f, w_out, cache_kt, cache_vt, cache_lft, layer, page_table, bp, tp, bs, fw,
              final):
    e = N_HEAD_A * HD_A
    n_s = bs * N_NEW
    q_p, kt_p, vt_p, z_p, gates_p = fox_proj(xp, nw, w_in, bp, tp, TM_PROMPT)
    q_s, kt_s, vt_s, z_s, gates_s = fox_proj(xs, nw, w_in, 1, n_s, n_s)
    qb, qa, kbt, kat, vbt, lft_p = fox_prep(q_p, kt_p, vt_p, gates_p, b_f, bp, tp, FOX_PREP_TB)
    o_p = fox_attention_prompt(qb, qa, kbt, kat, vbt, bp, tp, FOX_TQ)
    yp = out_proj(o_p, 0, z_p, 0, w_out, xp, fw, TM_PROMPT, final)
    qbd = _block_diag_queries(q_s.reshape(bs, N_NEW, e), N_HEAD_A, HD_A)
    gnt = _new_token_major(gates_s[:, :N_HEAD_A].T, bs)
    o_s, lfn = fox_attention_sample(qbd, _new_token_major(kt_s[0], bs), _new_token_major(vt_s[0], bs), gnt, b_f,
                                    cache_kt, cache_vt, cache_lft, layer, page_table)
    ys = out_proj(o_s.reshape(n_s, e), 0, z_s, 0, w_out, xs, fw, n_s, final)
    heads_t = lambda a_t, b_, t_: a_t.reshape(b_, N_HEAD_A, HD_A, t_).transpose(0, 3, 1, 2)
    new_p = (heads_t(kt_p, bp, tp), heads_t(vt_p, bp, tp), lft_p.transpose(0, 2, 1))
    new_s = (kt_s[0].T.reshape(bs, N_NEW, N_HEAD_A, HD_A), vt_s[0].T.reshape(bs, N_NEW, N_HEAD_A, HD_A),
             lfn[:, :, :N_NEW].transpose(0, 2, 1))
    return yp, ys, new_p, new_s


def dil_layer(xp, xs, nw, w_in, w_out, caches, bp, tp, bs):
    e = N_HEAD_B * HD_B
    no_gate = jnp.zeros((D_MODEL, 1), F32)
    proj_p, _ = norm_proj(xp, nw, w_in, no_gate, TM_PROJ, TN_PROJ)
    proj_s, _ = norm_proj(xs, nw, w_in, no_gate, xs.shape[0], TN_PROJ)
    z_col = 3 * len(DIL_GROUPS)
    res_p = [dil_band_attention(proj_p, g, bp, tp) for g in range(len(DIL_GROUPS))]
    yp = dil_out_proj([r[0] for r in res_p], [r[1] for r in res_p], proj_p, z_col, w_out, xp, TM_PROMPT)
    ps3 = proj_s.reshape(bs, N_NEW, -1)
    res_s, new_p, new_s = [], [], []
    for g, (window, _) in enumerate(DIL_GROUPS):
        part = lambda c: ps3[:, :, (3 * g + c) * e:(3 * g + c + 1) * e]
        buf_t = caches[g].transpose(0, 2, 3, 4, 1).reshape(bs, 2, e, window)
        knt = jnp.pad(part(1).transpose(0, 2, 1), ((0, 0), (0, 0), (0, 128 - N_NEW)))
        vnt = jnp.pad(part(2).transpose(0, 2, 1), ((0, 0), (0, 0), (0, 128 - N_NEW)))
        o_s, lse_s, new_buf_t = dil_sample_attention(_block_diag_queries(part(0), N_HEAD_B, HD_B), knt, vnt, buf_t, g)
        res_s.append((o_s, lse_s))
        keep = min(window, tp)
        rows = proj_p.reshape(bp, tp, -1)[:, tp - keep:, (3 * g + 1) * e:(3 * g + 3) * e]
        new_p.append(rows.reshape(bp, keep, 2, N_HEAD_B, HD_B))
        new_s.append(new_buf_t.reshape(bs, 2, N_HEAD_B, HD_B, window).transpose(0, 4, 1, 2, 3))
    flat = lambda a: a.reshape(bs * N_NEW, e)
    ys = dil_out_proj([flat(r[0]) for r in res_s], [flat(r[1]) for r in res_s], proj_s, z_col, w_out, xs,
                      xs.shape[0])
    return yp, ys, tuple(new_p), tuple(new_s)


def mlstm_layer(xp, xs, nw, w_in, b_i, b_f, conv_w, conv_b, norm_w, w_out, conv_state, c0, n0, m0, bp, tp, bs, fw):
    n_main = 5 * E_C
    w_main, w_gate = w_in[:, :n_main], w_in[:, n_main:]
    proj_p, gates_p = norm_proj(xp, nw, w_main, w_gate, TM_PROJ, TN_PROJ)
    proj_s, gates_s = norm_proj(xs, nw, w_main, w_gate, xs.shape[0], TN_PROJ)
    zeros = lambda *shape: jnp.zeros(shape, F32)
    hn_p, c_p, n_p, m_p = mlstm_mix(proj_p, gates_p, b_i, b_f, conv_w, conv_b, norm_w,
                                    zeros(bp, CONV_W - 1, 2 * E_C), zeros(bp, N_HEAD_C, DH_C, DH_C),
                                    zeros(bp, N_HEAD_C, DH_C), zeros(bp, N_HEAD_C), bp, tp, MLSTM_LC, MLSTM_LC)
    yp = out_proj(hn_p, 0, proj_p, 4, w_out, xp, fw, TM_PROMPT, False)
    pad_t = lambda a: jnp.pad(a.reshape(bs, N_NEW, -1), ((0, 0), (0, SAMPLE_LC - N_NEW), (0, 0))).reshape(
        bs * SAMPLE_LC, -1)
    hn_s, c_s, n_s, m_s = mlstm_mix(pad_t(proj_s), pad_t(gates_s), b_i, b_f, conv_w, conv_b, norm_w,
                                    conv_state, c0, n0, m0, bs, SAMPLE_LC, SAMPLE_LC, N_NEW)
    hn_s = hn_s.reshape(bs, SAMPLE_LC, E_C)[:, :N_NEW].reshape(bs * N_NEW, E_C)
    ys = out_proj(hn_s, 0, proj_s, 4, w_out, xs, fw, xs.shape[0], False)
    qk_p = proj_p[:, :2 * E_C].reshape(bp, tp, 2 * E_C)
    qk_s = proj_s[:, :2 * E_C].reshape(bs, N_NEW, 2 * E_C)
    conv_p = jnp.concatenate([zeros(bp, CONV_W - 1, 2 * E_C), qk_p], axis=1)[:, tp:]
    conv_s = jnp.concatenate([conv_state, qk_s], axis=1)[:, N_NEW:]
    return yp, ys, (c_p, n_p, m_p, conv_p), (c_s, n_s, m_s, conv_s)


def kernel(x_prompt, x_sample, cache_fox_k, cache_fox_v, cache_fox_logf, cache_dil0_kv, cache_dil1_kv, cache_dil2_kv, state_mlstm_C, state_mlstm_n, state_mlstm_m, state_mlstm_conv, page_table, norm_w, final_norm_w, fox_w_in, fox_b_f, fox_w_out, dil_w_in, dil_w_out, mlstm_w_in, mlstm_b_i, mlstm_b_f, mlstm_conv_w, mlstm_conv_b, mlstm_norm_w, mlstm_w_out):
    bp, tp, d = x_prompt.shape
    bs, ts, _ = x_sample.shape
    assert ts == N_NEW and d == D_MODEL
    depth = norm_w.shape[0]
    dil_caches = (cache_dil0_kv, cache_dil1_kv, cache_dil2_kv)
    n_la, n_pool = cache_fox_k.shape[:2]
    fox_kt = cache_fox_k.transpose(0, 1, 3, 4, 2).reshape(n_la, n_pool, N_HEAD_A * HD_A, PAGE)
    fox_vt = cache_fox_v.transpose(0, 1, 3, 4, 2).reshape(n_la, n_pool, N_HEAD_A * HD_A, PAGE)
    fox_lft = cache_fox_logf.transpose(0, 1, 3, 2)
    xp = x_prompt.reshape(bp * tp, d)
    xs = x_sample.reshape(bs * ts, d)
    fox_p, fox_s, dil_p, dil_s, ml_p, ml_s = [], [], [], [], [], []
    for i in range(depth):
        j, kind = divmod(i, 3)
        final = i == depth - 1
        if kind == 0:
            xp, xs, new_p, new_s = fox_layer(xp, xs, norm_w[i], fox_w_in[j], fox_b_f[j], fox_w_out[j],
                                             fox_kt, fox_vt, fox_lft, j, page_table,
                                             bp, tp, bs, final_norm_w, final)
            fox_p.append(new_p)
            fox_s.append(new_s)
        elif kind == 1:
            assert not final
            xp, xs, new_p, new_s = dil_layer(xp, xs, norm_w[i], dil_w_in[j], dil_w_out[j],
                                             tuple(c[j] for c in dil_caches), bp, tp, bs)
            dil_p.append(new_p)
            dil_s.append(new_s)
        else:
            assert not final
            xp, xs, new_p, new_s = mlstm_layer(xp, xs, norm_w[i], mlstm_w_in[j], mlstm_b_i[j], mlstm_b_f[j],
                                               mlstm_conv_w[j], mlstm_conv_b[j], mlstm_norm_w[j], mlstm_w_out[j],
                                               state_mlstm_conv[j], state_mlstm_C[j], state_mlstm_n[j],
                                               state_mlstm_m[j], bp, tp, bs, final_norm_w)
            ml_p.append(new_p)
            ml_s.append(new_s)
    stk = lambda lst, f: jnp.stack([e[f] for e in lst], axis=0)
    return (xp.reshape(bp, tp, d), xs.reshape(bs, ts, d),
            stk(fox_p, 0), stk(fox_p, 1), stk(fox_p, 2),
            stk(fox_s, 0), stk(fox_s, 1), stk(fox_s, 2),
            stk(dil_p, 0), stk(dil_p, 1), stk(dil_p, 2),
            stk(dil_s, 0), stk(dil_s, 1), stk(dil_s, 2),
            stk(ml_p, 0), stk(ml_p, 1), stk(ml_p, 2), stk(ml_p, 3),
            stk(ml_s, 0), stk(ml_s, 1), stk(ml_s, 2), stk(ml_s, 3))
```

```python
import functools

import numpy as np
import jax
import jax.numpy as jnp
from jax import lax
from jax.experimental import pallas as pl
from jax.experimental.pallas import tpu as pltpu

F32 = jnp.float32
BF16 = jnp.bfloat16
NEG = -1e30
EPS = 1e-6

D_MODEL = 1024
PAGE = 128
N_HEAD_A = 16
HD_A = 64
N_HEAD_B = 16
HD_B = 64
DIL_GROUPS = ((128, 1), (512, 4), (2048, 16))
BAND = 128
N_HEAD_C = 4
E_C = 2 * D_MODEL
DH_C = E_C // N_HEAD_C
CONV_W = 4
N_NEW = 4
ROWS = N_NEW * N_HEAD_A
VMEM_LIMIT = 56 * 1024 * 1024


def _cparams(sem):
    return pltpu.CompilerParams(dimension_semantics=sem, vmem_limit_bytes=VMEM_LIMIT)


def _split3(x):
    hi = x.astype(BF16)
    r = x - hi.astype(F32)
    mid = r.astype(BF16)
    lo = (r - mid.astype(F32)).astype(BF16)
    return hi, mid, lo


def _dot(a, b):
    return jnp.dot(a, b, preferred_element_type=F32)


def _dot_nt(a, b):
    return lax.dot_general(a, b, (((1,), (1,)), ((), ())), preferred_element_type=F32)


def _dot_tn(a, b):
    return lax.dot_general(a, b, (((0,), (0,)), ((), ())), preferred_element_type=F32)


def _dot3_left(x, sel):
    hi, mid, lo = _split3(x)
    return _dot(hi, sel) + _dot(mid, sel) + _dot(lo, sel)


def _dot3_right(sel, x):
    hi, mid, lo = _split3(x)
    return _dot(sel, hi) + _dot(sel, mid) + _dot(sel, lo)


def _log_sigmoid(x):
    return jnp.minimum(x, 0.0) - jnp.log1p(jnp.exp(-jnp.abs(x)))


def _silu(z):
    return z / (1.0 + jnp.exp(-z))


def _sigmoid(z):
    return 1.0 / (1.0 + jnp.exp(-z))


def _head_mask(n_head, hd):
    h_idx = np.arange(n_head * hd) // hd
    return (h_idx[None, :] == np.arange(n_head)[:, None]).astype(np.float32)


def _norm_proj_kernel(x_ref, nw_ref, w_ref, wgh_ref, wgl_ref, o_ref, g_ref, h_ref):
    @pl.when(pl.program_id(1) == 0)
    def _():
        x = x_ref[...]
        h = x * lax.rsqrt(jnp.mean(x * x, axis=-1, keepdims=True) + EPS) * nw_ref[...]
        hb = h.astype(BF16)
        hl = (h - hb.astype(F32)).astype(BF16)
        h_ref[...] = hb
        g_ref[...] = _dot(hb, wgh_ref[...]) + _dot(hb, wgl_ref[...]) + _dot(hl, wgh_ref[...])

    o_ref[...] = _dot(h_ref[...], w_ref[...])


def norm_proj(x, nw, w, wg, tm, tn):
    n, d = x.shape
    nout = w.shape[1]
    wgp = jnp.zeros((d, 128), F32).at[:, : wg.shape[1]].set(wg)
    wgh = wgp.astype(BF16)
    wgl = (wgp - wgh.astype(F32)).astype(BF16)
    return pl.pallas_call(
        _norm_proj_kernel,
        grid=(n // tm, nout // tn),
        in_specs=[
            pl.BlockSpec((tm, d), lambda i, j: (i, 0)),
            pl.BlockSpec((1, d), lambda i, j: (0, 0)),
            pl.BlockSpec((d, tn), lambda i, j: (0, j)),
            pl.BlockSpec((d, 128), lambda i, j: (0, 0)),
            pl.BlockSpec((d, 128), lambda i, j: (0, 0)),
        ],
        out_specs=[
            pl.BlockSpec((tm, tn), lambda i, j: (i, j)),
            pl.BlockSpec((tm, 128), lambda i, j: (i, 0)),
        ],
        out_shape=[jax.ShapeDtypeStruct((n, nout), F32), jax.ShapeDtypeStruct((n, 128), F32)],
        scratch_shapes=[pltpu.VMEM((tm, d), BF16)],
        compiler_params=_cparams(("parallel", "arbitrary")),
        name="norm_proj",
    )(x, nw.reshape(1, d), w.astype(BF16), wgh, wgl)


def _out_proj_kernel(a_ref, z_ref, w_ref, x_ref, fw_ref, o_ref, *, final_norm):
    u = (a_ref[...] * _silu(z_ref[...])).astype(BF16)
    y = x_ref[...] + _dot(u, w_ref[...])
    if final_norm:
        y = y * lax.rsqrt(jnp.mean(y * y, axis=-1, keepdims=True) + EPS) * fw_ref[...]
    o_ref[...] = y


def out_proj(a, a_col, z, z_col, w, x, fw, tm, final_norm):
    n, d = x.shape
    e = w.shape[0]
    return pl.pallas_call(
        functools.partial(_out_proj_kernel, final_norm=final_norm),
        grid=(n // tm,),
        in_specs=[
            pl.BlockSpec((tm, e), lambda i: (i, a_col)),
            pl.BlockSpec((tm, e), lambda i: (i, z_col)),
            pl.BlockSpec((e, d), lambda i: (0, 0)),
            pl.BlockSpec((tm, d), lambda i: (i, 0)),
            pl.BlockSpec((1, d), lambda i: (0, 0)),
        ],
        out_specs=pl.BlockSpec((tm, d), lambda i: (i, 0)),
        out_shape=jax.ShapeDtypeStruct((n, d), F32),
        compiler_params=_cparams(("parallel",)),
        name="out_proj",
    )(a, z, w.astype(BF16), x, fw.reshape(1, d))


def _dil_out_proj_kernel(o0_ref, o1_ref, o2_ref, l0_ref, l1_ref, l2_ref, z_ref, w_ref, x_ref, o_ref):
    l0, l1, l2 = l0_ref[...], l1_ref[...], l2_ref[...]
    m = jnp.maximum(jnp.maximum(l0, l1), l2)
    e0, e1, e2 = jnp.exp(l0 - m), jnp.exp(l1 - m), jnp.exp(l2 - m)
    o = (e0 * o0_ref[...] + e1 * o1_ref[...] + e2 * o2_ref[...]) / (e0 + e1 + e2)
    u = (o * _silu(z_ref[...])).astype(BF16)
    o_ref[...] = x_ref[...] + _dot(u, w_ref[...])


def dil_out_proj(os_, ls_, z, z_col, w, x, tm):
    n, d = x.shape
    e = w.shape[0]
    blk = pl.BlockSpec((tm, e), lambda i: (i, 0))
    return pl.pallas_call(
        _dil_out_proj_kernel,
        grid=(n // tm,),
        in_specs=[blk] * 6 + [
            pl.BlockSpec((tm, e), lambda i: (i, z_col)),
            pl.BlockSpec((e, d), lambda i: (0, 0)),
            pl.BlockSpec((tm, d), lambda i: (i, 0)),
        ],
        out_specs=pl.BlockSpec((tm, d), lambda i: (i, 0)),
        out_shape=jax.ShapeDtypeStruct((n, d), F32),
        compiler_params=_cparams(("parallel",)),
        name="dil_out_proj",
    )(*os_, *ls_, z, w.astype(BF16), x)


def _fox_proj_kernel(x_ref, nw_ref, w_ref, wgh_ref, wgl_ref, q_ref, kt_ref, vt_ref, z_ref, g_ref, h_ref):
    j = pl.program_id(1)

    @pl.when(j == 0)
    def _():
        x = x_ref[...]
        h = x * lax.rsqrt(jnp.mean(x * x, axis=-1, keepdims=True) + EPS) * nw_ref[...]
        hb = h.astype(BF16)
        hl = (h - hb.astype(F32)).astype(BF16)
        h_ref[...] = hb
        g_ref[...] = _dot_nt(hb, wgh_ref[...]) + _dot_nt(hb, wgl_ref[...]) + _dot_nt(hl, wgh_ref[...])
        q_ref[...] = _dot_nt(hb, w_ref[0])

    @pl.when(j == 1)
    def _():
        kt_ref[0] = _dot_nt(w_ref[0], h_ref[...])

    @pl.when(j == 2)
    def _():
        vt_ref[0] = _dot_nt(w_ref[0], h_ref[...])

    @pl.when(j == 3)
    def _():
        z_ref[...] = _dot_nt(h_ref[...], w_ref[0])


def fox_proj(x, nw, w_in, bsz, t, tm):
    n, d = x.shape
    e = N_HEAD_A * HD_A
    nblk = t // tm
    wt = w_in.T
    w4 = jnp.stack([wt[:e], wt[e:2 * e], wt[2 * e:3 * e], wt[3 * e + N_HEAD_A:]], axis=0).astype(BF16)
    wg = jnp.zeros((128, d), F32).at[:N_HEAD_A].set(wt[3 * e:3 * e + N_HEAD_A])
    wgh = wg.astype(BF16)
    wgl = (wg - wgh.astype(F32)).astype(BF16)
    row_blk = pl.BlockSpec((tm, e), lambda i, j: (i, 0))
    t_blk = pl.BlockSpec((1, e, tm), lambda i, j: (i // nblk, 0, i % nblk))
    return pl.pallas_call(
        _fox_proj_kernel,
        grid=(n // tm, 4),
        in_specs=[
            pl.BlockSpec((tm, d), lambda i, j: (i, 0)),
            pl.BlockSpec((1, d), lambda i, j: (0, 0)),
            pl.BlockSpec((1, e, d), lambda i, j: (j, 0, 0)),
            pl.BlockSpec((128, d), lambda i, j: (0, 0)),
            pl.BlockSpec((128, d), lambda i, j: (0, 0)),
        ],
        out_specs=[row_blk, t_blk, t_blk, row_blk, pl.BlockSpec((tm, 128), lambda i, j: (i, 0))],
        out_shape=[jax.ShapeDtypeStruct((n, e), F32), jax.ShapeDtypeStruct((bsz, e, t), F32),
                   jax.ShapeDtypeStruct((bsz, e, t), F32), jax.ShapeDtypeStruct((n, e), F32),
                   jax.ShapeDtypeStruct((n, 128), F32)],
        scratch_shapes=[pltpu.VMEM((tm, d), BF16)],
        compiler_params=_cparams(("parallel", "arbitrary")),
        name="fox_proj",
    )(x, nw.reshape(1, d), w4, wgh, wgl)


LOG2E = 1.4426950408889634
AUG_Q_C = (0, 1, 2)
AUG_K_C = (3, 4, 5)
AUG_PER_HEAD = 6
ONES_LANE = N_HEAD_A

def _fox_sel_matrices():
    selq = np.zeros((3, 128, D_MODEL), np.float32)
    selk = np.zeros((3, 128, D_MODEL), np.float32)
    for hp in range(N_HEAD_A // 2):
        for hh in range(2):
            base = hp * 128 + hh * AUG_PER_HEAD
            for p in range(3):
                selq[p, 2 * hp + hh, base + AUG_Q_C[p]] = 1.0
                selk[p, 2 * hp + hh, base + AUG_K_C[p]] = -1.0
                selq[0, ONES_LANE, base + AUG_K_C[p]] = 1.0
                selk[0, ONES_LANE, base + AUG_Q_C[p]] = 1.0
    return selq, np.ascontiguousarray(selk.transpose(0, 2, 1))


def _fox_prep_kernel(q_ref, kt_ref, vt_ref, g_ref, bf_ref, selq_ref, selkt_ref,
                     qb_ref, qa_ref, kbt_ref, kat_ref, vbt_ref, lft_ref, carry_ref, *, tb):
    @pl.when(pl.program_id(1) == 0)
    def _():
        carry_ref[...] = jnp.zeros_like(carry_ref)

    lf = _log_sigmoid(g_ref[...] + bf_ref[...])
    lft_ref[0] = lf.T[:N_HEAD_A]
    row = lax.broadcasted_iota(jnp.int32, (tb, tb), 0)
    col = lax.broadcasted_iota(jnp.int32, (tb, tb), 1)
    tril = (col <= row).astype(BF16)
    c = _dot3_right(tril, lf) + carry_ref[...]
    carry_ref[...] = c[tb - 1:tb, :]
    lane = lax.broadcasted_iota(jnp.int32, c.shape, 1)
    c1 = jnp.where(lane == ONES_LANE, 1.0, c * LOG2E)
    parts = _split3(c1)
    parts_t = _split3(c1.T)
    qa = _dot(parts[0], selq_ref[0])
    kat = _dot(selkt_ref[0], parts_t[0])
    for p in range(1, 3):
        qa = qa + _dot(parts[p], selq_ref[p])
        kat = kat + _dot(selkt_ref[p], parts_t[p])
    qa_ref[...] = qa.astype(BF16)
    kat_ref[0] = kat.astype(BF16)
    qb_ref[...] = (q_ref[...] * (HD_A ** -0.5 * LOG2E)).astype(BF16)
    kbt_ref[0] = kt_ref[0].astype(BF16)
    vbt_ref[0] = vt_ref[0].astype(BF16)


def fox_prep(q, kt, vt, gates, b_f, bsz, t, tb):
    n = bsz * t
    nb = t // tb
    e = N_HEAD_A * HD_A
    selq, selkt = _fox_sel_matrices()
    bfp = jnp.zeros((1, 128), F32).at[0, :N_HEAD_A].set(b_f)
    row_blk = pl.BlockSpec((tb, e), lambda b, i: (b * nb + i, 0))
    t_blk = pl.BlockSpec((1, e, tb), lambda b, i: (b, 0, i))
    g_blk = pl.BlockSpec((tb, 128), lambda b, i: (b * nb + i, 0))
    return pl.pallas_call(
        functools.partial(_fox_prep_kernel, tb=tb),
        grid=(bsz, nb),
        in_specs=[row_blk, t_blk, t_blk, g_blk, pl.BlockSpec((1, 128), lambda b, i: (0, 0)),
                  pl.BlockSpec((3, 128, e), lambda b, i: (0, 0, 0)),
                  pl.BlockSpec((3, e, 128), lambda b, i: (0, 0, 0))],
        out_specs=[row_blk, row_blk, t_blk, t_blk, t_blk,
                   pl.BlockSpec((1, N_HEAD_A, tb), lambda b, i: (b, 0, i))],
        out_shape=[jax.ShapeDtypeStruct((n, e), BF16)] * 2 + [jax.ShapeDtypeStruct((bsz, e, t), BF16)] * 3
        + [jax.ShapeDtypeStruct((bsz, N_HEAD_A, t), F32)],
        scratch_shapes=[pltpu.VMEM((1, 128), F32)],
        compiler_params=_cparams(("parallel", "arbitrary")),
        name="fox_prep",
    )(q, kt, vt, gates, bfp, jnp.asarray(selq, BF16), jnp.asarray(selkt, BF16))


def _fox_attn_kernel(qi_ref, kj_ref, qb_ref, qa_ref, kbt_ref, kat_ref, vbt_ref, o_ref,
                     lhs_ref, m_ref, acc_ref, *, tq):
    s_idx = pl.program_id(2)
    qi = qi_ref[s_idx]
    kj = kj_ref[s_idx]

    @pl.when(kj == 0)
    def _init():
        qcat = jnp.concatenate([qb_ref[...], qa_ref[...]], axis=1)
        lane = lax.broadcasted_iota(jnp.int32, (1, 256), 1)
        for hh in range(2):
            lo = 128 + hh * AUG_PER_HEAD
            keep = ((lane >= hh * 64) & (lane < hh * 64 + 64)) | ((lane >= lo) & (lane < lo + AUG_PER_HEAD))
            lhs_ref[hh] = jnp.where(keep, qcat, jnp.zeros_like(qcat))
        m_ref[...] = jnp.full_like(m_ref, NEG)
        acc_ref[...] = jnp.zeros_like(acc_ref)

    def step(diagonal):
        kcat = jnp.concatenate([kbt_ref[0], kat_ref[0]], axis=0)
        v = vbt_ref[0]
        vrow = lax.broadcasted_iota(jnp.int32, v.shape, 0)
        for hh in range(2):
            vh = jnp.where((vrow >= hh * 64) & (vrow < hh * 64 + 64), v, jnp.ones_like(v))
            s = _dot(lhs_ref[hh], kcat)
            if diagonal:
                row = lax.broadcasted_iota(jnp.int32, s.shape, 0)
                col = lax.broadcasted_iota(jnp.int32, s.shape, 1)
                s = jnp.where(col <= row, s, NEG)
            m_prev = m_ref[hh]
            m_new = jnp.maximum(m_prev, jnp.max(s, axis=1, keepdims=True))
            p = jnp.exp2(s - jnp.tile(m_new, (1, s.shape[1] // 128)))
            acc_ref[hh] = jnp.exp2(m_prev - m_new) * acc_ref[hh] + _dot_nt(p.astype(BF16), vh)
            m_ref[hh] = m_new

    @pl.when(kj < qi)
    def _off():
        step(False)

    @pl.when(kj == qi)
    def _diag():
        step(True)
        lane = lax.broadcasted_iota(jnp.int32, (tq, 128), 1)
        a0, a1 = acc_ref[0], acc_ref[1]
        o_ref[...] = jnp.where(lane < 64, a0 / pltpu.roll(a0, 64, 1), a1 / pltpu.roll(a1, 64, 1))


def fox_attention_prompt(qb, qa, kbt, kat, vbt, bsz, t, tq):
    n = bsz * t
    nq = t // tq
    qi = np.concatenate([np.full(i + 1, i, np.int32) for i in range(nq)])
    kj = np.concatenate([np.arange(i + 1, dtype=np.int32) for i in range(nq)])
    q_blk = pl.BlockSpec((tq, 128), lambda b, hp, s, qi_r, kj_r: (b * nq + qi_r[s], hp))
    k_blk = pl.BlockSpec((1, 128, tq), lambda b, hp, s, qi_r, kj_r: (b, hp, kj_r[s]))
    return pl.pallas_call(
        functools.partial(_fox_attn_kernel, tq=tq),
        grid_spec=pltpu.PrefetchScalarGridSpec(
            num_scalar_prefetch=2,
            grid=(bsz, N_HEAD_A // 2, len(qi)),
            in_specs=[q_blk, q_blk, k_blk, k_blk, k_blk],
            out_specs=q_blk,
            scratch_shapes=[pltpu.VMEM((2, tq, 256), BF16), pltpu.VMEM((2, tq, 128), F32),
                            pltpu.VMEM((2, tq, 128), F32)],
        ),
        out_shape=jax.ShapeDtypeStruct((n, D_MODEL), F32),
        compiler_params=_cparams(("parallel", "parallel", "arbitrary")),
        name="fox_attn",
    )(jnp.asarray(qi), jnp.asarray(kj), qb, qa, kbt, kat, vbt)


def _flash_update(q_ref, kt, vt, bias, valid, m_ref, l_ref, acc_ref):
    s = _dot(q_ref[0], kt.astype(BF16)) + bias
    if valid is not None:
        s = jnp.where(valid, s, NEG)
    m_prev = m_ref[...]
    m_new = jnp.maximum(m_prev, jnp.max(s, axis=1, keepdims=True))
    alpha = jnp.exp(m_prev - m_new)
    pr = jnp.exp(s - m_new)
    l_ref[...] = alpha * l_ref[...] + jnp.sum(pr, axis=1, keepdims=True)
    acc_ref[...] = alpha * acc_ref[...] + _dot_nt(pr.astype(BF16), vt.astype(BF16))
    m_ref[...] = m_new


def _head_diag(x, hmask):
    n_head = hmask.shape[0]
    return jnp.sum(x.reshape(N_NEW, n_head, x.shape[1]) * hmask[None], axis=1)


def _fox_sample_kernel(pt_ref, q_ref, *refs, n_groups, gp):
    k_refs, v_refs, lf_refs = refs[:gp], refs[gp:2 * gp], refs[2 * gp:3 * gp]
    (kn_ref, vn_ref, gn_ref, bf_ref, hmask_ref, o_ref, lfn_ref,
     m_ref, l_ref, acc_ref, carry_ref, ecol_ref) = refs[3 * gp:]
    p = pl.program_id(1)
    w = gp * PAGE

    @pl.when(p == 0)
    def _new_tokens():
        m_ref[...] = jnp.full_like(m_ref, NEG)
        l_ref[...] = jnp.zeros_like(l_ref)
        acc_ref[...] = jnp.zeros_like(acc_ref)
        carry_ref[...] = jnp.zeros_like(carry_ref)
        lfn = _log_sigmoid(gn_ref[0] + bf_ref[...])
        lfn_ref[0] = lfn
        r_i = lax.broadcasted_iota(jnp.int32, (128, 128), 0)
        c_i = lax.broadcasted_iota(jnp.int32, (128, 128), 1)
        incl = ((r_i <= c_i) & (r_i < N_NEW)).astype(BF16)
        e = _dot3_left(lfn, incl)
        e4 = jnp.concatenate([e] * N_NEW, axis=0)
        tok = lax.broadcasted_iota(jnp.int32, (ROWS, 128), 0) // N_HEAD_A
        lane = lax.broadcasted_iota(jnp.int32, (ROWS, 128), 1)
        ecol = jnp.sum(jnp.where(lane == tok, e4, 0.0), axis=1, keepdims=True)
        ecol_ref[...] = ecol
        valid = (lane <= tok) & (lane < N_NEW)
        _flash_update(q_ref, kn_ref[0], vn_ref[0], ecol - e4, valid, m_ref, l_ref, acc_ref)

    @pl.when(p > 0)
    def _pages():
        cat = lambda rs: jnp.concatenate([r[0, 0] for r in rs], axis=1) if gp > 1 else rs[0][0, 0]
        lf = cat(lf_refs)
        r_i = lax.broadcasted_iota(jnp.int32, (w, w), 0)
        c_i = lax.broadcasted_iota(jnp.int32, (w, w), 1)
        after = (r_i > c_i).astype(BF16)
        d = _dot3_left(lf, after) + carry_ref[...]
        carry_ref[...] = carry_ref[...] + jnp.sum(lf, axis=1, keepdims=True)
        d4 = jnp.concatenate([d] * N_NEW, axis=0)
        _flash_update(q_ref, cat(k_refs), cat(v_refs), ecol_ref[...] + d4, None, m_ref, l_ref, acc_ref)

    @pl.when(p == n_groups)
    def _finish():
        o_ref[0] = _head_diag(acc_ref[...] / l_ref[...], hmask_ref[...])


FOX_PAGES_PER_STEP = 8


def fox_attention_sample(qbd, knt, vnt, gnt, b_f, cache_kt, cache_vt, cache_lft, layer, page_table):
    bs, n_pages = page_table.shape
    gp = FOX_PAGES_PER_STEP
    assert n_pages % gp == 0
    n_groups = n_pages // gp
    e = N_HEAD_A * HD_A

    def page_map(c):
        def index_map(b, p, pt):
            grp = n_groups - jnp.maximum(p, 1)
            return (layer, pt[b * n_pages + grp * gp + c], 0, 0)
        return index_map

    per_b = lambda shape: pl.BlockSpec(shape, lambda b, p, pt: (b, 0, 0))
    const2 = lambda shape: pl.BlockSpec(shape, lambda b, p, pt: (0, 0))
    kv_specs = [pl.BlockSpec((1, 1, e, PAGE), page_map(c)) for c in range(gp)]
    lf_specs = [pl.BlockSpec((1, 1, N_HEAD_A, PAGE), page_map(c)) for c in range(gp)]
    return pl.pallas_call(
        functools.partial(_fox_sample_kernel, n_groups=n_groups, gp=gp),
        grid_spec=pltpu.PrefetchScalarGridSpec(
            num_scalar_prefetch=1,
            grid=(bs, n_groups + 1),
            in_specs=[per_b((1, ROWS, e))] + kv_specs + kv_specs + lf_specs + [
                per_b((1, e, 128)), per_b((1, e, 128)), per_b((1, N_HEAD_A, 128)),
                const2((N_HEAD_A, 1)), const2((N_HEAD_A, e))],
            out_specs=[per_b((1, N_NEW, e)), per_b((1, N_HEAD_A, 128))],
            scratch_shapes=[pltpu.VMEM((ROWS, 1), F32), pltpu.VMEM((ROWS, 1), F32),
                            pltpu.VMEM((ROWS, e), F32), pltpu.VMEM((N_HEAD_A, 1), F32),
                            pltpu.VMEM((ROWS, 1), F32)],
        ),
        out_shape=[jax.ShapeDtypeStruct((bs, N_NEW, e), F32),
                   jax.ShapeDtypeStruct((bs, N_HEAD_A, 128), F32)],
        compiler_params=_cparams(("parallel", "arbitrary")),
        name="fox_sample",
    )(page_table.reshape(-1), qbd, *([cache_kt] * gp), *([cache_vt] * gp), *([cache_lft] * gp),
      knt, vnt, gnt, b_f.reshape(N_HEAD_A, 1), jnp.asarray(_head_mask(N_HEAD_A, HD_A)))


def _alibi_slope(h):
    return float(2.0 ** (-8.0 * (h + 1.0) / N_HEAD_B))


def _shift_in(x, nxt):
    cw = x.shape[1]
    y = pltpu.roll(x, cw - N_NEW, 1)
    tail = pltpu.roll(nxt, 128 - N_NEW, 1)
    lane = lax.broadcasted_iota(jnp.int32, (x.shape[0], 128), 1)
    last = jnp.where(lane < 128 - N_NEW, y[:, cw - 128:], tail)
    return last if cw == 128 else jnp.concatenate([y[:, :cw - 128], last], axis=1)


def _dil_sample_kernel(q_ref, k_ref, v_ref, kn_ref, vn_ref, slope_ref, hmask_ref, o_ref, lse_ref, buf_ref,
                       m_ref, l_ref, acc_ref, nxt_ref, *, dil, window, cw, n_chunks):
    p = pl.program_id(1)
    tok = lax.broadcasted_iota(jnp.int32, (ROWS, cw), 0) // N_HEAD_B
    lane = lax.broadcasted_iota(jnp.int32, (ROWS, cw), 1)
    slope = slope_ref[...]

    @pl.when(p == 0)
    def _new_tokens():
        m_ref[...] = jnp.full_like(m_ref, NEG)
        l_ref[...] = jnp.zeros_like(l_ref)
        acc_ref[...] = jnp.zeros_like(acc_ref)
        nxt_ref[0] = kn_ref[0]
        nxt_ref[1] = vn_ref[0]
        tok_n = lax.broadcasted_iota(jnp.int32, (ROWS, 128), 0) // N_HEAD_B
        lane_n = lax.broadcasted_iota(jnp.int32, (ROWS, 128), 1)
        dist = tok_n - lane_n
        valid = (dist >= 0) & ((dist & (dil - 1)) == 0) & (lane_n < N_NEW)
        _flash_update(q_ref, kn_ref[0], vn_ref[0], -slope * dist.astype(F32), valid, m_ref, l_ref, acc_ref)

    @pl.when(p > 0)
    def _chunk():
        k, v = k_ref[0, 0], v_ref[0, 0]
        dist = window + tok - ((n_chunks - p) * cw + lane)
        valid = ((dist & (dil - 1)) == 0) & (dist <= window)
        _flash_update(q_ref, k, v, -slope * dist.astype(F32), valid, m_ref, l_ref, acc_ref)
        buf_ref[0, 0] = _shift_in(k, nxt_ref[0])
        buf_ref[0, 1] = _shift_in(v, nxt_ref[1])
        nxt_ref[0] = k[:, :128]
        nxt_ref[1] = v[:, :128]

    @pl.when(p == n_chunks)
    def _finish():
        hmask = hmask_ref[...]
        l = l_ref[...]
        o_ref[0] = _head_diag(acc_ref[...] / l, hmask)
        lse = m_ref[...] + jnp.log(l)
        lse_ref[0] = _head_diag(jnp.broadcast_to(lse, acc_ref.shape), hmask)


def dil_sample_attention(qbd, knt, vnt, buf_t, g):
    window, dil = DIL_GROUPS[g]
    assert dil & (dil - 1) == 0 and buf_t.shape[3] == window
    bs = qbd.shape[0]
    e = N_HEAD_B * HD_B
    cw = min(window, 512)
    n_chunks = window // cw
    slope = np.array([[_alibi_slope(r % N_HEAD_B)] for r in range(ROWS)], np.float32)
    per_b = lambda shape: pl.BlockSpec(shape, lambda b, p: (b, 0, 0))
    newest_first = lambda p: n_chunks - jnp.maximum(p, 1)
    chunk = lambda kv: pl.BlockSpec((1, 1, e, cw), lambda b, p, kv=kv: (b, kv, 0, newest_first(p)))
    return pl.pallas_call(
        functools.partial(_dil_sample_kernel, dil=dil, window=window, cw=cw, n_chunks=n_chunks),
        grid=(bs, n_chunks + 1),
        in_specs=[per_b((1, ROWS, e)), chunk(0), chunk(1), per_b((1, e, 128)), per_b((1, e, 128)),
                  pl.BlockSpec((ROWS, 1), lambda b, p: (0, 0)),
                  pl.BlockSpec((N_HEAD_B, e), lambda b, p: (0, 0))],
        out_specs=[per_b((1, N_NEW, e))] * 2 + [pl.BlockSpec((1, 2, e, cw), lambda b, p: (b, 0, 0, newest_first(p)))],
        out_shape=[jax.ShapeDtypeStruct((bs, N_NEW, e), F32)] * 2 + [jax.ShapeDtypeStruct(buf_t.shape, F32)],
        scratch_shapes=[pltpu.VMEM((ROWS, 1), F32), pltpu.VMEM((ROWS, 1), F32), pltpu.VMEM((ROWS, e), F32),
                        pltpu.VMEM((2, e, 128), F32)],
        compiler_params=_cparams(("parallel", "arbitrary")),
        name="dil_sample%d" % g,
    )(qbd, buf_t, buf_t, knt, vnt, jnp.asarray(slope), jnp.asarray(_head_mask(N_HEAD_B, HD_B)))


def _block_diag_queries(q, n_head, hd):
    bs = q.shape[0]
    qbd = (q * (hd ** -0.5))[:, :, None, :] * jnp.asarray(_head_mask(n_head, hd))[None, None]
    return qbd.reshape(bs, N_NEW * n_head, n_head * hd).astype(BF16)


def _new_token_major(x_t, bs):
    f = x_t.shape[0]
    return jnp.pad(x_t.reshape(f, bs, N_NEW).transpose(1, 0, 2), ((0, 0), (0, 0), (0, 128 - N_NEW)))


DIL_CHUNK_TOKENS = 2048
DIL_UNROLL = 4


def _dil_band_kernel(q_ref, kp_ref, kc_ref, vp_ref, vc_ref, slope_ref, o_ref, lse_ref, *, dil, nblk):
    n = pl.program_id(1)
    r = BAND
    span = r * dil
    qrow = lax.broadcasted_iota(jnp.int32, (r, 2 * r), 0)
    kcol = lax.broadcasted_iota(jnp.int32, (r, 2 * r), 1)
    dist = r + qrow - kcol
    in_band = (dist >= 0) & (dist <= r)
    distf = (dist * dil).astype(F32)
    lane = lax.broadcasted_iota(jnp.int32, (r, 128), 1)
    first = lane < HD_B

    def unit(blk, res):
        def rows(b_):
            return pl.ds(b_ * span + res, r, stride=dil) if dil > 1 else pl.ds(b_ * span, r)
        if blk == 0:
            k_prev, v_prev = kp_ref[rows(0), :], vp_ref[rows(0), :]
            valid = in_band & ((kcol >= r) | (n > 0))
        else:
            k_prev, v_prev = kc_ref[rows(blk - 1), :], vc_ref[rows(blk - 1), :]
            valid = in_band
        q = q_ref[rows(blk), :] * (HD_B ** -0.5)
        k2 = jnp.concatenate([k_prev, kc_ref[rows(blk), :]], axis=0).astype(BF16)
        v2 = jnp.concatenate([v_prev, vc_ref[rows(blk), :]], axis=0).astype(BF16)
        outs, lses = [], []
        for hh in range(2):
            qh = jnp.where(first == (hh == 0), q, 0.0).astype(BF16)
            s = _dot_nt(qh, k2) - slope_ref[0, hh:hh + 1, 0:1] * distf
            s = jnp.where(valid, s, NEG)
            m = jnp.broadcast_to(jnp.max(s, axis=1, keepdims=True), (r, 128))
            p = jnp.exp(s - jnp.tile(m, (1, 2)))
            l = jnp.broadcast_to(jnp.sum(p, axis=1, keepdims=True), (r, 128))
            outs.append(_dot(p.astype(BF16), v2) / l)
            lses.append(m + jnp.log(l))
        o_ref[rows(blk), :] = jnp.where(first, outs[0], outs[1])
        lse_ref[rows(blk), :] = jnp.where(first, lses[0], lses[1])

    for blk in range(nblk):
        if dil == 1:
            unit(blk, 0)
        else:
            def body(res, carry, blk=blk):
                unit(blk, res)
                return carry
            lax.fori_loop(0, dil, body, 0, unroll=min(dil, DIL_UNROLL))


def dil_band_attention(proj, g, bsz, t):
    window, dil = DIL_GROUPS[g]
    assert window // dil == BAND
    span = BAND * dil
    nblk = max(DIL_CHUNK_TOKENS // span, 1)
    ch = nblk * span
    nch = t // ch
    npair = N_HEAD_B // 2
    slopes = np.zeros((npair, 8, 128), np.float32)
    for h in range(N_HEAD_B):
        slopes[h // 2, h % 2, :] = _alibi_slope(h)

    def cur(c):
        return pl.BlockSpec((ch, 128), lambda b, n, hp, c=c: (b * nch + n, c * npair + hp))

    def prev(c):
        return pl.BlockSpec((span, 128),
                            lambda b, n, hp, c=c: (jnp.maximum((b * nch + n) * nblk - 1, 0), c * npair + hp))

    out_blk = pl.BlockSpec((ch, 128), lambda b, n, hp: (b * nch + n, hp))
    return pl.pallas_call(
        functools.partial(_dil_band_kernel, dil=dil, nblk=nblk),
        grid=(bsz, nch, npair),
        in_specs=[cur(3 * g), prev(3 * g + 1), cur(3 * g + 1), prev(3 * g + 2), cur(3 * g + 2),
                  pl.BlockSpec((1, 8, 128), lambda b, n, hp: (hp, 0, 0))],
        out_specs=[out_blk, out_blk],
        out_shape=[jax.ShapeDtypeStruct((bsz * t, D_MODEL), F32)] * 2,
        compiler_params=_cparams(("parallel", "arbitrary", "arbitrary")),
        name="dil_band%d" % g,
    )(proj, proj, proj, proj, proj, jnp.asarray(slopes))


def _mlstm_kernel(qp_ref, kp_ref, v_ref, op_ref, g_ref, gb_ref, cwq_ref, cwk_ref, cbq_ref, cbk_ref,
                  nw_ref, csq_ref, csk_ref, c0_ref, n0_ref, m0_ref,
                  hn_ref, c_out_ref, n_out_ref, m_out_ref,
                  c_s, n_s, m_s, xq_s, xk_s, *, lc, n_valid, nc):
    h = pl.program_id(1)
    c = pl.program_id(2)

    @pl.when(c == 0)
    def _():
        c_s[...] = c0_ref[0, 0]
        n_s[...] = n0_ref[0, 0]
        m_s[...] = m0_ref[0, 0]
        xq_s[0:8, :] = csq_ref[0]
        xk_s[0:8, :] = csk_ref[0]

    def conv(x_ref, xs, cw_ref, cb_ref):
        xs[8:8 + lc, :] = x_ref[...]
        acc = cb_ref[...] + cw_ref[0:1, :] * xs[pl.ds(8 - (CONV_W - 1), lc), :]
        for w in range(1, CONV_W):
            acc = acc + cw_ref[w:w + 1, :] * xs[pl.ds(8 - (CONV_W - 1) + w, lc), :]
        xs[0:8, :] = xs[lc:lc + 8, :]
        return _silu(acc)

    q = conv(qp_ref, xq_s, cwq_ref, cbq_ref)
    k = conv(kp_ref, xk_s, cwk_ref, cbk_ref) * (DH_C ** -0.5)
    v = v_ref[...]

    g = g_ref[...] + gb_ref[...]
    lane = lax.broadcasted_iota(jnp.int32, g.shape, 1)
    li_col = jnp.sum(jnp.where(lane == h, g, 0.0), axis=1, keepdims=True)
    lf_col = _log_sigmoid(jnp.sum(jnp.where(lane == N_HEAD_C + h, g, 0.0), axis=1, keepdims=True))
    if n_valid < lc:
        rows = lax.broadcasted_iota(jnp.int32, (lc, 1), 0)
        li_col = jnp.where(rows < n_valid, li_col, NEG)
        lf_col = jnp.where(rows < n_valid, lf_col, 0.0)
    row = lax.broadcasted_iota(jnp.int32, (lc, lc), 0)
    col = lax.broadcasted_iota(jnp.int32, (lc, lc), 1)
    eye = row == col
    tril = col <= row
    to_row = lambda x_col: jnp.sum(jnp.where(eye, x_col, 0.0), axis=0, keepdims=True)
    lf_row = to_row(lf_col)
    li_row = to_row(li_col)
    b_col = jnp.sum(jnp.where(tril, lf_row, 0.0), axis=1, keepdims=True)
    b_row = to_row(b_col)
    m_prev = m_s[...]
    d = jnp.where(tril, b_col - b_row + li_row, NEG)
    m_t = jnp.maximum(b_col + m_prev, jnp.max(d, axis=1, keepdims=True))
    dexp = jnp.where(tril, jnp.exp(d - m_t), 0.0)
    inter = jnp.exp(b_col + m_prev - m_t)
    qb, kb, vb = q.astype(BF16), k.astype(BF16), v.astype(BF16)
    w = _dot_nt(qb, kb) * dexp
    num = _dot(w.astype(BF16), vb) + inter * _dot(qb, c_s[...].astype(BF16))
    den = jnp.sum(w, axis=1, keepdims=True) + inter * jnp.sum(q * n_s[...], axis=1, keepdims=True)
    hh = num / jnp.maximum(jnp.abs(den), jnp.exp(-m_t))
    m_new = m_t[lc - 1:lc, :]
    b_last = b_col[lc - 1:lc, :]
    g_col = jnp.exp(b_last - b_col + li_col - m_new)
    decay = jnp.exp(b_last + m_prev - m_new)
    kg = k * g_col
    c_s[...] = decay * c_s[...] + _dot_tn(kg.astype(BF16), vb)
    n_s[...] = decay * n_s[...] + jnp.sum(kg, axis=0, keepdims=True)
    m_s[...] = m_new

    ho = hh * _sigmoid(op_ref[...])
    mu = jnp.mean(ho, axis=1, keepdims=True)
    var = jnp.mean(jnp.square(ho - mu), axis=1, keepdims=True)
    hn_ref[...] = (ho - mu) * lax.rsqrt(var + EPS) * nw_ref[...]

    @pl.when(c == nc - 1)
    def _():
        c_out_ref[0, 0] = c_s[...]
        n_out_ref[0, 0] = n_s[...]
        m_out_ref[0, 0] = m_s[...]


def mlstm_mix(proj, gates, b_i, b_f, conv_w, conv_b, norm_w, conv_state, c0, n0, m0, bsz, t, lc, n_valid):
    n = bsz * t
    nc = t // lc
    assert n_valid == lc or nc == 1
    nh = N_HEAD_C
    gb = jnp.zeros((1, 128), F32).at[0, :nh].set(b_i).at[0, nh:2 * nh].set(b_f)
    cs = jnp.pad(conv_state, ((0, 0), (8 - (CONV_W - 1), 0), (0, 0)))
    blk = lambda c0_: pl.BlockSpec((lc, DH_C), lambda b, h, c, c0_=c0_: (b * nc + c, c0_ + h))
    per_h = lambda rows, c0_: pl.BlockSpec((rows, DH_C), lambda b, h, c, c0_=c0_: (0, c0_ + h))
    st4 = lambda shape: pl.BlockSpec(shape, lambda b, h, c: (b, h, 0, 0))
    hn, c_out, n_out, m_out = pl.pallas_call(
        functools.partial(_mlstm_kernel, lc=lc, n_valid=n_valid, nc=nc),
        grid=(bsz, nh, nc),
        in_specs=[
            blk(0), blk(nh), blk(2 * nh), blk(3 * nh),
            pl.BlockSpec((lc, 128), lambda b, h, c: (b * nc + c, 0)),
            pl.BlockSpec((1, 128), lambda b, h, c: (0, 0)),
            per_h(CONV_W, 0), per_h(CONV_W, nh), per_h(1, 0), per_h(1, nh),
            per_h(1, 0),
            pl.BlockSpec((1, 8, DH_C), lambda b, h, c: (b, 0, h)),
            pl.BlockSpec((1, 8, DH_C), lambda b, h, c: (b, 0, nh + h)),
            st4((1, 1, DH_C, DH_C)), st4((1, 1, 1, DH_C)), st4((1, 1, 1, 1)),
        ],
        out_specs=[
            pl.BlockSpec((lc, DH_C), lambda b, h, c: (b * nc + c, h)),
            st4((1, 1, DH_C, DH_C)), st4((1, 1, 1, DH_C)), st4((1, 1, 1, 1)),
        ],
        out_shape=[
            jax.ShapeDtypeStruct((n, E_C), F32),
            jax.ShapeDtypeStruct((bsz, nh, DH_C, DH_C), F32),
            jax.ShapeDtypeStruct((bsz, nh, 1, DH_C), F32),
            jax.ShapeDtypeStruct((bsz, nh, 1, 1), F32),
        ],
        scratch_shapes=[pltpu.VMEM((DH_C, DH_C), F32), pltpu.VMEM((1, DH_C), F32), pltpu.VMEM((1, 1), F32),
                        pltpu.VMEM((lc + 8, DH_C), F32), pltpu.VMEM((lc + 8, DH_C), F32)],
        compiler_params=_cparams(("parallel", "parallel", "arbitrary")),
        name="mlstm",
    )(proj, proj, proj, proj, gates, gb, conv_w, conv_w, conv_b.reshape(1, -1), conv_b.reshape(1, -1),
      norm_w.reshape(1, -1), cs, cs, c0, n0.reshape(bsz, nh, 1, DH_C), m0.reshape(bsz, nh, 1, 1))
    return hn, c_out, n_out.reshape(bsz, nh, DH_C), m_out.reshape(bsz, nh)


TM_PROMPT = 512
TM_PROJ = 1024
TN_PROJ = 2048
FOX_TQ = 1024
FOX_PREP_TB = 512
MLSTM_LC = 256
SAMPLE_LC = 16


def fox_layer(xp, xs, nw, w_in, b_f, w_out, cache_kt, cache_vt, cache_lft, layer, page_table, bp, tp, bs, fw,
              final):
    e = N_HEAD_A * HD_A
    n_s = bs * N_NEW
    q_p, kt_p, vt_p, z_p, gates_p = fox_proj(xp, nw, w_in, bp, tp, TM_PROMPT)
    q_s, kt_s, vt_s, z_s, gates_s = fox_proj(xs, nw, w_in, 1, n_s, n_s)
    qb, qa, kbt, kat, vbt, lft_p = fox_prep(q_p, kt_p, vt_p, gates_p, b_f, bp, tp, FOX_PREP_TB)
    o_p = fox_attention_prompt(qb, qa, kbt, kat, vbt, bp, tp, FOX_TQ)
    yp = out_proj(o_p, 0, z_p, 0, w_out, xp, fw, TM_PROMPT, final)
    qbd = _block_diag_queries(q_s.reshape(bs, N_NEW, e), N_HEAD_A, HD_A)
    gnt = _new_token_major(gates_s[:, :N_HEAD_A].T, bs)
    o_s, lfn = fox_attention_sample(qbd, _new_token_major(kt_s[0], bs), _new_token_major(vt_s[0], bs), gnt, b_f,
                                    cache_kt, cache_vt, cache_lft, layer, page_table)
    ys = out_proj(o_s.reshape(n_s, e), 0, z_s, 0, w_out, xs, fw, n_s, final)
    heads_t = lambda a_t, b_, t_: a_t.reshape(b_, N_HEAD_A, HD_A, t_).transpose(0, 3, 1, 2)
    new_p = (heads_t(kt_p, bp, tp), heads_t(vt_p, bp, tp), lft_p.transpose(0, 2, 1))
    new_s = (kt_s[0].T.reshape(bs, N_NEW, N_HEAD_A, HD_A), vt_s[0].T.reshape(bs, N_NEW, N_HEAD_A, HD_A),
             lfn[:, :, :N_NEW].transpose(0, 2, 1))
    return yp, ys, new_p, new_s


def dil_layer(xp, xs, nw, w_in, w_out, caches, bp, tp, bs):
    e = N_HEAD_B * HD_B
    no_gate = jnp.zeros((D_MODEL, 1), F32)
    proj_p, _ = norm_proj(xp, nw, w_in, no_gate, TM_PROJ, TN_PROJ)
    proj_s, _ = norm_proj(xs, nw, w_in, no_gate, xs.shape[0], TN_PROJ)
    z_col = 3 * len(DIL_GROUPS)
    res_p = [dil_band_attention(proj_p, g, bp, tp) for g in range(len(DIL_GROUPS))]
    yp = dil_out_proj([r[0] for r in res_p], [r[1] for r in res_p], proj_p, z_col, w_out, xp, TM_PROMPT)
    ps3 = proj_s.reshape(bs, N_NEW, -1)
    res_s, new_p, new_s = [], [], []
    for g, (window, _) in enumerate(DIL_GROUPS):
        part = lambda c: ps3[:, :, (3 * g + c) * e:(3 * g + c + 1) * e]
        buf_t = caches[g].transpose(0, 2, 3, 4, 1).reshape(bs, 2, e, window)
        knt = jnp.pad(part(1).transpose(0, 2, 1), ((0, 0), (0, 0), (0, 128 - N_NEW)))
        vnt = jnp.pad(part(2).transpose(0, 2, 1), ((0, 0), (0, 0), (0, 128 - N_NEW)))
        o_s, lse_s, new_buf_t = dil_sample_attention(_block_diag_queries(part(0), N_HEAD_B, HD_B), knt, vnt, buf_t, g)
        res_s.append((o_s, lse_s))
        keep = min(window, tp)
        rows = proj_p.reshape(bp, tp, -1)[:, tp - keep:, (3 * g + 1) * e:(3 * g + 3) * e]
        new_p.append(rows.reshape(bp, keep, 2, N_HEAD_B, HD_B))
        new_s.append(new_buf_t.reshape(bs, 2, N_HEAD_B, HD_B, window).transpose(0, 4, 1, 2, 3))
    flat = lambda a: a.reshape(bs * N_NEW, e)
    ys = dil_out_proj([flat(r[0]) for r in res_s], [flat(r[1]) for r in res_s], proj_s, z_col, w_out, xs,
                      xs.shape[0])
    return yp, ys, tuple(new_p), tuple(new_s)


def mlstm_layer(xp, xs, nw, w_in, b_i, b_f, conv_w, conv_b, norm_w, w_out, conv_state, c0, n0, m0, bp, tp, bs, fw):
    n_main = 5 * E_C
    w_main, w_gate = w_in[:, :n_main], w_in[:, n_main:]
    proj_p, gates_p = norm_proj(xp, nw, w_main, w_gate, TM_PROJ, TN_PROJ)
    proj_s, gates_s = norm_proj(xs, nw, w_main, w_gate, xs.shape[0], TN_PROJ)
    zeros = lambda *shape: jnp.zeros(shape, F32)
    hn_p, c_p, n_p, m_p = mlstm_mix(proj_p, gates_p, b_i, b_f, conv_w, conv_b, norm_w,
                                    zeros(bp, CONV_W - 1, 2 * E_C), zeros(bp, N_HEAD_C, DH_C, DH_C),
                                    zeros(bp, N_HEAD_C, DH_C), zeros(bp, N_HEAD_C), bp, tp, MLSTM_LC, MLSTM_LC)
    yp = out_proj(hn_p, 0, proj_p, 4, w_out, xp, fw, TM_PROMPT, False)
    pad_t = lambda a: jnp.pad(a.reshape(bs, N_NEW, -1), ((0, 0), (0, SAMPLE_LC - N_NEW), (0, 0))).reshape(
        bs * SAMPLE_LC, -1)
    hn_s, c_s, n_s, m_s = mlstm_mix(pad_t(proj_s), pad_t(gates_s), b_i, b_f, conv_w, conv_b, norm_w,
                                    conv_state, c0, n0, m0, bs, SAMPLE_LC, SAMPLE_LC, N_NEW)
    hn_s = hn_s.reshape(bs, SAMPLE_LC, E_C)[:, :N_NEW].reshape(bs * N_NEW, E_C)
    ys = out_proj(hn_s, 0, proj_s, 4, w_out, xs, fw, xs.shape[0], False)
    keep_p = min(tp, CONV_W - 1)
    qk_p = proj_p.reshape(bp, tp, -1)[:, tp - keep_p:, :2 * E_C]
    conv_p = jnp.concatenate([zeros(bp, CONV_W - 1 - keep_p, 2 * E_C), qk_p], axis=1)
    qk_s = proj_s[:, :2 * E_C].reshape(bs, N_NEW, 2 * E_C)
    conv_s = jnp.concatenate([conv_state, qk_s], axis=1)[:, N_NEW:]
    return yp, ys, (c_p, n_p, m_p, conv_p), (c_s, n_s, m_s, conv_s)


def kernel(x_prompt, x_sample, cache_fox_k, cache_fox_v, cache_fox_logf, cache_dil0_kv, cache_dil1_kv, cache_dil2_kv, state_mlstm_C, state_mlstm_n, state_mlstm_m, state_mlstm_conv, page_table, norm_w, final_norm_w, fox_w_in, fox_b_f, fox_w_out, dil_w_in, dil_w_out, mlstm_w_in, mlstm_b_i, mlstm_b_f, mlstm_conv_w, mlstm_conv_b, mlstm_norm_w, mlstm_w_out):
    bp, tp, d = x_prompt.shape
    bs, ts, _ = x_sample.shape
    assert ts == N_NEW and d == D_MODEL
    depth = norm_w.shape[0]
    dil_caches = (cache_dil0_kv, cache_dil1_kv, cache_dil2_kv)
    n_la, n_pool = cache_fox_k.shape[:2]
    fox_kt = cache_fox_k.transpose(0, 1, 3, 4, 2).reshape(n_la, n_pool, N_HEAD_A * HD_A, PAGE)
    fox_vt = cache_fox_v.transpose(0, 1, 3, 4, 2).reshape(n_la, n_pool, N_HEAD_A * HD_A, PAGE)
    fox_lft = cache_fox_logf.transpose(0, 1, 3, 2)
    xp = x_prompt.reshape(bp * tp, d)
    xs = x_sample.reshape(bs * ts, d)
    fox_p, fox_s, dil_p, dil_s, ml_p, ml_s = [], [], [], [], [], []
    for i in range(depth):
        j, kind = divmod(i, 3)
        final = i == depth - 1
        if kind == 0:
            xp, xs, new_p, new_s = fox_layer(xp, xs, norm_w[i], fox_w_in[j], fox_b_f[j], fox_w_out[j],
                                             fox_kt, fox_vt, fox_lft, j, page_table,
                                             bp, tp, bs, final_norm_w, final)
            fox_p.append(new_p)
            fox_s.append(new_s)
        elif kind == 1:
            assert not final
            xp, xs, new_p, new_s = dil_layer(xp, xs, norm_w[i], dil_w_in[j], dil_w_out[j],
                                             tuple(c[j] for c in dil_caches), bp, tp, bs)
            dil_p.append(new_p)
            dil_s.append(new_s)
        else:
            assert not final
            xp, xs, new_p, new_s = mlstm_layer(xp, xs, norm_w[i], mlstm_w_in[j], mlstm_b_i[j], mlstm_b_f[j],
                                               mlstm_conv_w[j], mlstm_conv_b[j], mlstm_norm_w[j], mlstm_w_out[j],
                                               state_mlstm_conv[j], state_mlstm_C[j], state_mlstm_n[j],
                                               state_mlstm_m[j], bp, tp, bs, final_norm_w)
            ml_p.append(new_p)
            ml_s.append(new_s)
    stk = lambda lst, f: jnp.stack([e[f] for e in lst], axis=0)
    return (xp.reshape(bp, tp, d), xs.reshape(bs, ts, d),
            stk(fox_p, 0), stk(fox_p, 1), stk(fox_p, 2),
            stk(fox_s, 0), stk(fox_s, 1), stk(fox_s, 2),
            stk(dil_p, 0), stk(dil_p, 1), stk(dil_p, 2),
            stk(dil_s, 0), stk(dil_s, 1), stk(dil_s, 2),
            stk(ml_p, 0), stk(ml_p, 1), stk(ml_p, 2), stk(ml_p, 3),
            stk(ml_s, 0), stk(ml_s, 1), stk(ml_s, 2), stk(ml_s, 3))
```

```python
import functools

import numpy as np
import jax
import jax.numpy as jnp
from jax import lax
from jax.experimental import pallas as pl
from jax.experimental.pallas import tpu as pltpu

F32 = jnp.float32
BF16 = jnp.bfloat16
NEG = -1e30
EPS = 1e-6

D_MODEL = 1024
PAGE = 128
N_HEAD_A = 16
HD_A = 64
N_HEAD_B = 16
HD_B = 64
DIL_GROUPS = ((128, 1), (512, 4), (2048, 16))
BAND = 128
N_HEAD_C = 4
E_C = 2 * D_MODEL
DH_C = E_C // N_HEAD_C
CONV_W = 4
N_NEW = 4
ROWS = N_NEW * N_HEAD_A
VMEM_LIMIT = 56 * 1024 * 1024


def _cparams(sem):
    return pltpu.CompilerParams(dimension_semantics=sem, vmem_limit_bytes=VMEM_LIMIT)


def _split3(x):
    hi = x.astype(BF16)
    r = x - hi.astype(F32)
    mid = r.astype(BF16)
    lo = (r - mid.astype(F32)).astype(BF16)
    return hi, mid, lo


def _dot(a, b):
    return jnp.dot(a, b, preferred_element_type=F32)


def _dot_nt(a, b):
    return lax.dot_general(a, b, (((1,), (1,)), ((), ())), preferred_element_type=F32)


def _dot_tn(a, b):
    return lax.dot_general(a, b, (((0,), (0,)), ((), ())), preferred_element_type=F32)


def _dot3_left(x, sel):
    hi, mid, lo = _split3(x)
    return _dot(hi, sel) + _dot(mid, sel) + _dot(lo, sel)


def _dot3_right(sel, x):
    hi, mid, lo = _split3(x)
    return _dot(sel, hi) + _dot(sel, mid) + _dot(sel, lo)


def _log_sigmoid(x):
    return jnp.minimum(x, 0.0) - jnp.log1p(jnp.exp(-jnp.abs(x)))


def _silu(z):
    return z / (1.0 + jnp.exp(-z))


def _sigmoid(z):
    return 1.0 / (1.0 + jnp.exp(-z))


def _head_mask(n_head, hd):
    h_idx = np.arange(n_head * hd) // hd
    return (h_idx[None, :] == np.arange(n_head)[:, None]).astype(np.float32)


def _norm_proj_kernel(x_ref, nw_ref, w_ref, wgh_ref, wgl_ref, o_ref, g_ref, h_ref):
    @pl.when(pl.program_id(1) == 0)
    def _():
        x = x_ref[...]
        h = x * lax.rsqrt(jnp.mean(x * x, axis=-1, keepdims=True) + EPS) * nw_ref[...]
        hb = h.astype(BF16)
        hl = (h - hb.astype(F32)).astype(BF16)
        h_ref[...] = hb
        g_ref[...] = _dot(hb, wgh_ref[...]) + _dot(hb, wgl_ref[...]) + _dot(hl, wgh_ref[...])

    o_ref[...] = _dot(h_ref[...], w_ref[...])


def norm_proj(x, nw, w, wg, tm, tn):
    n, d = x.shape
    nout = w.shape[1]
    wgp = jnp.zeros((d, 128), F32).at[:, : wg.shape[1]].set(wg)
    wgh = wgp.astype(BF16)
    wgl = (wgp - wgh.astype(F32)).astype(BF16)
    return pl.pallas_call(
        _norm_proj_kernel,
        grid=(n // tm, nout // tn),
        in_specs=[
            pl.BlockSpec((tm, d), lambda i, j: (i, 0)),
            pl.BlockSpec((1, d), lambda i, j: (0, 0)),
            pl.BlockSpec((d, tn), lambda i, j: (0, j)),
            pl.BlockSpec((d, 128), lambda i, j: (0, 0)),
            pl.BlockSpec((d, 128), lambda i, j: (0, 0)),
        ],
        out_specs=[
            pl.BlockSpec((tm, tn), lambda i, j: (i, j)),
            pl.BlockSpec((tm, 128), lambda i, j: (i, 0)),
        ],
        out_shape=[jax.ShapeDtypeStruct((n, nout), F32), jax.ShapeDtypeStruct((n, 128), F32)],
        scratch_shapes=[pltpu.VMEM((tm, d), BF16)],
        compiler_params=_cparams(("parallel", "arbitrary")),
        name="norm_proj",
    )(x, nw.reshape(1, d), w.astype(BF16), wgh, wgl)


def _out_proj_kernel(a_ref, z_ref, w_ref, x_ref, fw_ref, o_ref, *, final_norm):
    u = (a_ref[...] * _silu(z_ref[...])).astype(BF16)
    y = x_ref[...] + _dot(u, w_ref[...])
    if final_norm:
        y = y * lax.rsqrt(jnp.mean(y * y, axis=-1, keepdims=True) + EPS) * fw_ref[...]
    o_ref[...] = y


def out_proj(a, a_col, z, z_col, w, x, fw, tm, final_norm):
    n, d = x.shape
    e = w.shape[0]
    return pl.pallas_call(
        functools.partial(_out_proj_kernel, final_norm=final_norm),
        grid=(n // tm,),
        in_specs=[
            pl.BlockSpec((tm, e), lambda i: (i, a_col)),
            pl.BlockSpec((tm, e), lambda i: (i, z_col)),
            pl.BlockSpec((e, d), lambda i: (0, 0)),
            pl.BlockSpec((tm, d), lambda i: (i, 0)),
            pl.BlockSpec((1, d), lambda i: (0, 0)),
        ],
        out_specs=pl.BlockSpec((tm, d), lambda i: (i, 0)),
        out_shape=jax.ShapeDtypeStruct((n, d), F32),
        compiler_params=_cparams(("parallel",)),
        name="out_proj",
    )(a, z, w.astype(BF16), x, fw.reshape(1, d))


def _dil_out_proj_kernel(o0_ref, o1_ref, o2_ref, l0_ref, l1_ref, l2_ref, z_ref, w_ref, x_ref, o_ref):
    l0, l1, l2 = l0_ref[...], l1_ref[...], l2_ref[...]
    m = jnp.maximum(jnp.maximum(l0, l1), l2)
    e0, e1, e2 = jnp.exp(l0 - m), jnp.exp(l1 - m), jnp.exp(l2 - m)
    o = (e0 * o0_ref[...] + e1 * o1_ref[...] + e2 * o2_ref[...]) / (e0 + e1 + e2)
    u = (o * _silu(z_ref[...])).astype(BF16)
    o_ref[...] = x_ref[...] + _dot(u, w_ref[...])


def dil_out_proj(os_, ls_, z, z_col, w, x, tm):
    n, d = x.shape
    e = w.shape[0]
    blk = pl.BlockSpec((tm, e), lambda i: (i, 0))
    return pl.pallas_call(
        _dil_out_proj_kernel,
        grid=(n // tm,),
        in_specs=[blk] * 6 + [
            pl.BlockSpec((tm, e), lambda i: (i, z_col)),
            pl.BlockSpec((e, d), lambda i: (0, 0)),
            pl.BlockSpec((tm, d), lambda i: (i, 0)),
        ],
        out_specs=pl.BlockSpec((tm, d), lambda i: (i, 0)),
        out_shape=jax.ShapeDtypeStruct((n, d), F32),
        compiler_params=_cparams(("parallel",)),
        name="dil_out_proj",
    )(*os_, *ls_, z, w.astype(BF16), x)


def _fox_proj_kernel(x_ref, nw_ref, w_ref, wgh_ref, wgl_ref, q_ref, kt_ref, vt_ref, z_ref, g_ref, h_ref):
    j = pl.program_id(1)

    @pl.when(j == 0)
    def _():
        x = x_ref[...]
        h = x * lax.rsqrt(jnp.mean(x * x, axis=-1, keepdims=True) + EPS) * nw_ref[...]
        hb = h.astype(BF16)
        hl = (h - hb.astype(F32)).astype(BF16)
        h_ref[...] = hb
        g_ref[...] = _dot_nt(hb, wgh_ref[...]) + _dot_nt(hb, wgl_ref[...]) + _dot_nt(hl, wgh_ref[...])
        q_ref[...] = _dot_nt(hb, w_ref[0])

    @pl.when(j == 1)
    def _():
        kt_ref[0] = _dot_nt(w_ref[0], h_ref[...])

    @pl.when(j == 2)
    def _():
        vt_ref[0] = _dot_nt(w_ref[0], h_ref[...])

    @pl.when(j == 3)
    def _():
        z_ref[...] = _dot_nt(h_ref[...], w_ref[0])


def fox_proj(x, nw, w_in, bsz, t, tm):
    n, d = x.shape
    e = N_HEAD_A * HD_A
    nblk = t // tm
    wt = w_in.T
    w4 = jnp.stack([wt[:e], wt[e:2 * e], wt[2 * e:3 * e], wt[3 * e + N_HEAD_A:]], axis=0).astype(BF16)
    wg = jnp.zeros((128, d), F32).at[:N_HEAD_A].set(wt[3 * e:3 * e + N_HEAD_A])
    wgh = wg.astype(BF16)
    wgl = (wg - wgh.astype(F32)).astype(BF16)
    row_blk = pl.BlockSpec((tm, e), lambda i, j: (i, 0))
    t_blk = pl.BlockSpec((1, e, tm), lambda i, j: (i // nblk, 0, i % nblk))
    return pl.pallas_call(
        _fox_proj_kernel,
        grid=(n // tm, 4),
        in_specs=[
            pl.BlockSpec((tm, d), lambda i, j: (i, 0)),
            pl.BlockSpec((1, d), lambda i, j: (0, 0)),
            pl.BlockSpec((1, e, d), lambda i, j: (j, 0, 0)),
            pl.BlockSpec((128, d), lambda i, j: (0, 0)),
            pl.BlockSpec((128, d), lambda i, j: (0, 0)),
        ],
        out_specs=[row_blk, t_blk, t_blk, row_blk, pl.BlockSpec((tm, 128), lambda i, j: (i, 0))],
        out_shape=[jax.ShapeDtypeStruct((n, e), F32), jax.ShapeDtypeStruct((bsz, e, t), F32),
                   jax.ShapeDtypeStruct((bsz, e, t), F32), jax.ShapeDtypeStruct((n, e), F32),
                   jax.ShapeDtypeStruct((n, 128), F32)],
        scratch_shapes=[pltpu.VMEM((tm, d), BF16)],
        compiler_params=_cparams(("parallel", "arbitrary")),
        name="fox_proj",
    )(x, nw.reshape(1, d), w4, wgh, wgl)


LOG2E = 1.4426950408889634
AUG_Q_C = (0, 1, 2)
AUG_K_C = (3, 4, 5)
AUG_PER_HEAD = 6
ONES_LANE = N_HEAD_A

def _fox_sel_matrices():
    selq = np.zeros((3, 128, D_MODEL), np.float32)
    selk = np.zeros((3, 128, D_MODEL), np.float32)
    for hp in range(N_HEAD_A // 2):
        for hh in range(2):
            base = hp * 128 + hh * AUG_PER_HEAD
            for p in range(3):
                selq[p, 2 * hp + hh, base + AUG_Q_C[p]] = 1.0
                selk[p, 2 * hp + hh, base + AUG_K_C[p]] = -1.0
                selq[0, ONES_LANE, base + AUG_K_C[p]] = 1.0
                selk[0, ONES_LANE, base + AUG_Q_C[p]] = 1.0
    return selq, np.ascontiguousarray(selk.transpose(0, 2, 1))


def _fox_prep_kernel(q_ref, kt_ref, vt_ref, g_ref, bf_ref, selq_ref, selkt_ref,
                     qb_ref, qa_ref, kbt_ref, kat_ref, vbt_ref, lft_ref, carry_ref, *, tb):
    @pl.when(pl.program_id(1) == 0)
    def _():
        carry_ref[...] = jnp.zeros_like(carry_ref)

    lf = _log_sigmoid(g_ref[...] + bf_ref[...])
    lft_ref[0] = lf.T[:N_HEAD_A]
    row = lax.broadcasted_iota(jnp.int32, (tb, tb), 0)
    col = lax.broadcasted_iota(jnp.int32, (tb, tb), 1)
    tril = (col <= row).astype(BF16)
    c = _dot3_right(tril, lf) + carry_ref[...]
    carry_ref[...] = c[tb - 1:tb, :]
    lane = lax.broadcasted_iota(jnp.int32, c.shape, 1)
    c1 = jnp.where(lane == ONES_LANE, 1.0, c * LOG2E)
    parts = _split3(c1)
    parts_t = _split3(c1.T)
    qa = _dot(parts[0], selq_ref[0])
    kat = _dot(selkt_ref[0], parts_t[0])
    for p in range(1, 3):
        qa = qa + _dot(parts[p], selq_ref[p])
        kat = kat + _dot(selkt_ref[p], parts_t[p])
    qa_ref[...] = qa.astype(BF16)
    kat_ref[0] = kat.astype(BF16)
    qb_ref[...] = (q_ref[...] * (HD_A ** -0.5 * LOG2E)).astype(BF16)
    kbt_ref[0] = kt_ref[0].astype(BF16)
    vbt_ref[0] = vt_ref[0].astype(BF16)


def fox_prep(q, kt, vt, gates, b_f, bsz, t, tb):
    n = bsz * t
    nb = t // tb
    e = N_HEAD_A * HD_A
    selq, selkt = _fox_sel_matrices()
    bfp = jnp.zeros((1, 128), F32).at[0, :N_HEAD_A].set(b_f)
    row_blk = pl.BlockSpec((tb, e), lambda b, i: (b * nb + i, 0))
    t_blk = pl.BlockSpec((1, e, tb), lambda b, i: (b, 0, i))
    g_blk = pl.BlockSpec((tb, 128), lambda b, i: (b * nb + i, 0))
    return pl.pallas_call(
        functools.partial(_fox_prep_kernel, tb=tb),
        grid=(bsz, nb),
        in_specs=[row_blk, t_blk, t_blk, g_blk, pl.BlockSpec((1, 128), lambda b, i: (0, 0)),
                  pl.BlockSpec((3, 128, e), lambda b, i: (0, 0, 0)),
                  pl.BlockSpec((3, e, 128), lambda b, i: (0, 0, 0))],
        out_specs=[row_blk, row_blk, t_blk, t_blk, t_blk,
                   pl.BlockSpec((1, N_HEAD_A, tb), lambda b, i: (b, 0, i))],
        out_shape=[jax.ShapeDtypeStruct((n, e), BF16)] * 2 + [jax.ShapeDtypeStruct((bsz, e, t), BF16)] * 3
        + [jax.ShapeDtypeStruct((bsz, N_HEAD_A, t), F32)],
        scratch_shapes=[pltpu.VMEM((1, 128), F32)],
        compiler_params=_cparams(("parallel", "arbitrary")),
        name="fox_prep",
    )(q, kt, vt, gates, bfp, jnp.asarray(selq, BF16), jnp.asarray(selkt, BF16))


def _fox_attn_kernel(qi_ref, kj_ref, qb_ref, qa_ref, kbt_ref, kat_ref, vbt_ref, o_ref,
                     lhs_ref, m_ref, acc_ref, *, tq):
    s_idx = pl.program_id(2)
    qi = qi_ref[s_idx]
    kj = kj_ref[s_idx]

    @pl.when(kj == 0)
    def _init():
        qcat = jnp.concatenate([qb_ref[...], qa_ref[...]], axis=1)
        lane = lax.broadcasted_iota(jnp.int32, (1, 256), 1)
        for hh in range(2):
            lo = 128 + hh * AUG_PER_HEAD
            keep = ((lane >= hh * 64) & (lane < hh * 64 + 64)) | ((lane >= lo) & (lane < lo + AUG_PER_HEAD))
            lhs_ref[hh] = jnp.where(keep, qcat, jnp.zeros_like(qcat))
        m_ref[...] = jnp.full_like(m_ref, NEG)
        acc_ref[...] = jnp.zeros_like(acc_ref)

    def step(diagonal):
        kcat = jnp.concatenate([kbt_ref[0], kat_ref[0]], axis=0)
        v = vbt_ref[0]
        vrow = lax.broadcasted_iota(jnp.int32, v.shape, 0)
        for hh in range(2):
            vh = jnp.where((vrow >= hh * 64) & (vrow < hh * 64 + 64), v, jnp.ones_like(v))
            s = _dot(lhs_ref[hh], kcat)
            if diagonal:
                row = lax.broadcasted_iota(jnp.int32, s.shape, 0)
                col = lax.broadcasted_iota(jnp.int32, s.shape, 1)
                s = jnp.where(col <= row, s, NEG)
            m_prev = m_ref[hh]
            m_new = jnp.maximum(m_prev, jnp.max(s, axis=1, keepdims=True))
            p = jnp.exp2(s - jnp.tile(m_new, (1, s.shape[1] // 128)))
            acc_ref[hh] = jnp.exp2(m_prev - m_new) * acc_ref[hh] + _dot_nt(p.astype(BF16), vh)
            m_ref[hh] = m_new

    @pl.when(kj < qi)
    def _off():
        step(False)

    @pl.when(kj == qi)
    def _diag():
        step(True)
        lane = lax.broadcasted_iota(jnp.int32, (tq, 128), 1)
        a0, a1 = acc_ref[0], acc_ref[1]
        o_ref[...] = jnp.where(lane < 64, a0 / pltpu.roll(a0, 64, 1), a1 / pltpu.roll(a1, 64, 1))


def fox_attention_prompt(qb, qa, kbt, kat, vbt, bsz, t, tq):
    n = bsz * t
    nq = t // tq
    qi = np.concatenate([np.full(i + 1, i, np.int32) for i in range(nq)])
    kj = np.concatenate([np.arange(i + 1, dtype=np.int32) for i in range(nq)])
    q_blk = pl.BlockSpec((tq, 128), lambda b, hp, s, qi_r, kj_r: (b * nq + qi_r[s], hp))
    k_blk = pl.BlockSpec((1, 128, tq), lambda b, hp, s, qi_r, kj_r: (b, hp, kj_r[s]))
    return pl.pallas_call(
        functools.partial(_fox_attn_kernel, tq=tq),
        grid_spec=pltpu.PrefetchScalarGridSpec(
            num_scalar_prefetch=2,
            grid=(bsz, N_HEAD_A // 2, len(qi)),
            in_specs=[q_blk, q_blk, k_blk, k_blk, k_blk],
            out_specs=q_blk,
            scratch_shapes=[pltpu.VMEM((2, tq, 256), BF16), pltpu.VMEM((2, tq, 128), F32),
                            pltpu.VMEM((2, tq, 128), F32)],
        ),
        out_shape=jax.ShapeDtypeStruct((n, D_MODEL), F32),
        compiler_params=_cparams(("parallel", "parallel", "arbitrary")),
        name="fox_attn",
    )(jnp.asarray(qi), jnp.asarray(kj), qb, qa, kbt, kat, vbt)


def _flash_update(q_ref, kt, vt, bias, valid, m_ref, l_ref, acc_ref):
    s = _dot(q_ref[0], kt.astype(BF16)) + bias
    if valid is not None:
        s = jnp.where(valid, s, NEG)
    m_prev = m_ref[...]
    m_new = jnp.maximum(m_prev, jnp.max(s, axis=1, keepdims=True))
    alpha = jnp.exp(m_prev - m_new)
    pr = jnp.exp(s - m_new)
    l_ref[...] = alpha * l_ref[...] + jnp.sum(pr, axis=1, keepdims=True)
    acc_ref[...] = alpha * acc_ref[...] + _dot_nt(pr.astype(BF16), vt.astype(BF16))
    m_ref[...] = m_new


def _head_diag(x, hmask):
    n_head = hmask.shape[0]
    return jnp.sum(x.reshape(N_NEW, n_head, x.shape[1]) * hmask[None], axis=1)


def _fox_sample_kernel(pt_ref, q_ref, *refs, n_groups, gp):
    k_refs, v_refs, lf_refs = refs[:gp], refs[gp:2 * gp], refs[2 * gp:3 * gp]
    (kn_ref, vn_ref, gn_ref, bf_ref, hmask_ref, o_ref, lfn_ref,
     m_ref, l_ref, acc_ref, carry_ref, ecol_ref) = refs[3 * gp:]
    p = pl.program_id(1)
    w = gp * PAGE

    @pl.when(p == 0)
    def _new_tokens():
        m_ref[...] = jnp.full_like(m_ref, NEG)
        l_ref[...] = jnp.zeros_like(l_ref)
        acc_ref[...] = jnp.zeros_like(acc_ref)
        carry_ref[...] = jnp.zeros_like(carry_ref)
        lfn = _log_sigmoid(gn_ref[0] + bf_ref[...])
        lfn_ref[0] = lfn
        r_i = lax.broadcasted_iota(jnp.int32, (128, 128), 0)
        c_i = lax.broadcasted_iota(jnp.int32, (128, 128), 1)
        incl = ((r_i <= c_i) & (r_i < N_NEW)).astype(BF16)
        e = _dot3_left(lfn, incl)
        e4 = jnp.concatenate([e] * N_NEW, axis=0)
        tok = lax.broadcasted_iota(jnp.int32, (ROWS, 128), 0) // N_HEAD_A
        lane = lax.broadcasted_iota(jnp.int32, (ROWS, 128), 1)
        ecol = jnp.sum(jnp.where(lane == tok, e4, 0.0), axis=1, keepdims=True)
        ecol_ref[...] = ecol
        valid = (lane <= tok) & (lane < N_NEW)
        _flash_update(q_ref, kn_ref[0], vn_ref[0], ecol - e4, valid, m_ref, l_ref, acc_ref)

    @pl.when(p > 0)
    def _pages():
        cat = lambda rs: jnp.concatenate([r[0, 0] for r in rs], axis=1) if gp > 1 else rs[0][0, 0]
        lf = cat(lf_refs)
        r_i = lax.broadcasted_iota(jnp.int32, (w, w), 0)
        c_i = lax.broadcasted_iota(jnp.int32, (w, w), 1)
        after = (r_i > c_i).astype(BF16)
        d = _dot3_left(lf, after) + carry_ref[...]
        carry_ref[...] = carry_ref[...] + jnp.sum(lf, axis=1, keepdims=True)
        d4 = jnp.concatenate([d] * N_NEW, axis=0)
        _flash_update(q_ref, cat(k_refs), cat(v_refs), ecol_ref[...] + d4, None, m_ref, l_ref, acc_ref)

    @pl.when(p == n_groups)
    def _finish():
        o_ref[0] = _head_diag(acc_ref[...] / l_ref[...], hmask_ref[...])


FOX_PAGES_PER_STEP = 8


def fox_attention_sample(qbd, knt, vnt, gnt, b_f, cache_kt, cache_vt, cache_lft, layer, page_table):
    bs, n_pages = page_table.shape
    gp = FOX_PAGES_PER_STEP
    assert n_pages % gp == 0
    n_groups = n_pages // gp
    e = N_HEAD_A * HD_A

    def page_map(c):
        def index_map(b, p, pt):
            grp = n_groups - jnp.maximum(p, 1)
            return (layer, pt[b * n_pages + grp * gp + c], 0, 0)
        return index_map

    per_b = lambda shape: pl.BlockSpec(shape, lambda b, p, pt: (b, 0, 0))
    const2 = lambda shape: pl.BlockSpec(shape, lambda b, p, pt: (0, 0))
    kv_specs = [pl.BlockSpec((1, 1, e, PAGE), page_map(c)) for c in range(gp)]
    lf_specs = [pl.BlockSpec((1, 1, N_HEAD_A, PAGE), page_map(c)) for c in range(gp)]
    return pl.pallas_call(
        functools.partial(_fox_sample_kernel, n_groups=n_groups, gp=gp),
        grid_spec=pltpu.PrefetchScalarGridSpec(
            num_scalar_prefetch=1,
            grid=(bs, n_groups + 1),
            in_specs=[per_b((1, ROWS, e))] + kv_specs + kv_specs + lf_specs + [
                per_b((1, e, 128)), per_b((1, e, 128)), per_b((1, N_HEAD_A, 128)),
                const2((N_HEAD_A, 1)), const2((N_HEAD_A, e))],
            out_specs=[per_b((1, N_NEW, e)), per_b((1, N_HEAD_A, 128))],
            scratch_shapes=[pltpu.VMEM((ROWS, 1), F32), pltpu.VMEM((ROWS, 1), F32),
                            pltpu.VMEM((ROWS, e), F32), pltpu.VMEM((N_HEAD_A, 1), F32),
                            pltpu.VMEM((ROWS, 1), F32)],
        ),
        out_shape=[jax.ShapeDtypeStruct((bs, N_NEW, e), F32),
                   jax.ShapeDtypeStruct((bs, N_HEAD_A, 128), F32)],
        compiler_params=_cparams(("parallel", "arbitrary")),
        name="fox_sample",
    )(page_table.reshape(-1), qbd, *([cache_kt] * gp), *([cache_vt] * gp), *([cache_lft] * gp),
      knt, vnt, gnt, b_f.reshape(N_HEAD_A, 1), jnp.asarray(_head_mask(N_HEAD_A, HD_A)))


def _alibi_slope(h):
    return float(2.0 ** (-8.0 * (h + 1.0) / N_HEAD_B))


def _shift_in(x, nxt):
    cw = x.shape[1]
    y = pltpu.roll(x, cw - N_NEW, 1)
    tail = pltpu.roll(nxt, 128 - N_NEW, 1)
    lane = lax.broadcasted_iota(jnp.int32, (x.shape[0], 128), 1)
    last = jnp.where(lane < 128 - N_NEW, y[:, cw - 128:], tail)
    return last if cw == 128 else jnp.concatenate([y[:, :cw - 128], last], axis=1)


def _dil_sample_kernel(q_ref, k_ref, v_ref, kn_ref, vn_ref, slope_ref, hmask_ref, o_ref, lse_ref, buf_ref,
                       m_ref, l_ref, acc_ref, nxt_ref, *, dil, window, cw, n_chunks):
    p = pl.program_id(1)
    tok = lax.broadcasted_iota(jnp.int32, (ROWS, cw), 0) // N_HEAD_B
    lane = lax.broadcasted_iota(jnp.int32, (ROWS, cw), 1)
    slope = slope_ref[...]

    @pl.when(p == 0)
    def _new_tokens():
        m_ref[...] = jnp.full_like(m_ref, NEG)
        l_ref[...] = jnp.zeros_like(l_ref)
        acc_ref[...] = jnp.zeros_like(acc_ref)
        nxt_ref[0] = kn_ref[0]
        nxt_ref[1] = vn_ref[0]
        tok_n = lax.broadcasted_iota(jnp.int32, (ROWS, 128), 0) // N_HEAD_B
        lane_n = lax.broadcasted_iota(jnp.int32, (ROWS, 128), 1)
        dist = tok_n - lane_n
        valid = (dist >= 0) & ((dist & (dil - 1)) == 0) & (lane_n < N_NEW)
        _flash_update(q_ref, kn_ref[0], vn_ref[0], -slope * dist.astype(F32), valid, m_ref, l_ref, acc_ref)

    @pl.when(p > 0)
    def _chunk():
        k, v = k_ref[0, 0], v_ref[0, 0]
        dist = window + tok - ((n_chunks - p) * cw + lane)
        valid = ((dist & (dil - 1)) == 0) & (dist <= window)
        _flash_update(q_ref, k, v, -slope * dist.astype(F32), valid, m_ref, l_ref, acc_ref)
        buf_ref[0, 0] = _shift_in(k, nxt_ref[0])
        buf_ref[0, 1] = _shift_in(v, nxt_ref[1])
        nxt_ref[0] = k[:, :128]
        nxt_ref[1] = v[:, :128]

    @pl.when(p == n_chunks)
    def _finish():
        hmask = hmask_ref[...]
        l = l_ref[...]
        o_ref[0] = _head_diag(acc_ref[...] / l, hmask)
        lse = m_ref[...] + jnp.log(l)
        lse_ref[0] = _head_diag(jnp.broadcast_to(lse, acc_ref.shape), hmask)


def dil_sample_attention(qbd, knt, vnt, buf_t, g):
    window, dil = DIL_GROUPS[g]
    assert dil & (dil - 1) == 0 and buf_t.shape[3] == window
    bs = qbd.shape[0]
    e = N_HEAD_B * HD_B
    cw = min(window, 512)
    n_chunks = window // cw
    slope = np.array([[_alibi_slope(r % N_HEAD_B)] for r in range(ROWS)], np.float32)
    per_b = lambda shape: pl.BlockSpec(shape, lambda b, p: (b, 0, 0))
    newest_first = lambda p: n_chunks - jnp.maximum(p, 1)
    chunk = lambda kv: pl.BlockSpec((1, 1, e, cw), lambda b, p, kv=kv: (b, kv, 0, newest_first(p)))
    return pl.pallas_call(
        functools.partial(_dil_sample_kernel, dil=dil, window=window, cw=cw, n_chunks=n_chunks),
        grid=(bs, n_chunks + 1),
        in_specs=[per_b((1, ROWS, e)), chunk(0), chunk(1), per_b((1, e, 128)), per_b((1, e, 128)),
                  pl.BlockSpec((ROWS, 1), lambda b, p: (0, 0)),
                  pl.BlockSpec((N_HEAD_B, e), lambda b, p: (0, 0))],
        out_specs=[per_b((1, N_NEW, e))] * 2 + [pl.BlockSpec((1, 2, e, cw), lambda b, p: (b, 0, 0, newest_first(p)))],
        out_shape=[jax.ShapeDtypeStruct((bs, N_NEW, e), F32)] * 2 + [jax.ShapeDtypeStruct(buf_t.shape, F32)],
        scratch_shapes=[pltpu.VMEM((ROWS, 1), F32), pltpu.VMEM((ROWS, 1), F32), pltpu.VMEM((ROWS, e), F32),
                        pltpu.VMEM((2, e, 128), F32)],
        compiler_params=_cparams(("parallel", "arbitrary")),
        name="dil_sample%d" % g,
    )(qbd, buf_t, buf_t, knt, vnt, jnp.asarray(slope), jnp.asarray(_head_mask(N_HEAD_B, HD_B)))


def _block_diag_queries(q, n_head, hd):
    bs = q.shape[0]
    qbd = (q * (hd ** -0.5))[:, :, None, :] * jnp.asarray(_head_mask(n_head, hd))[None, None]
    return qbd.reshape(bs, N_NEW * n_head, n_head * hd).astype(BF16)


def _new_token_major(x_t, bs):
    f = x_t.shape[0]
    return jnp.pad(x_t.reshape(f, bs, N_NEW).transpose(1, 0, 2), ((0, 0), (0, 0), (0, 128 - N_NEW)))


DIL_CHUNK_TOKENS = 2048
DIL_UNROLL = 4


def _dil_band_kernel(q_ref, kp_ref, kc_ref, vp_ref, vc_ref, slope_ref, o_ref, lse_ref, *, dil, nblk):
    n = pl.program_id(1)
    r = BAND
    span = r * dil
    qrow = lax.broadcasted_iota(jnp.int32, (r, 2 * r), 0)
    kcol = lax.broadcasted_iota(jnp.int32, (r, 2 * r), 1)
    dist = r + qrow - kcol
    in_band = (dist >= 0) & (dist <= r)
    distf = (dist * dil).astype(F32)
    lane = lax.broadcasted_iota(jnp.int32, (r, 128), 1)
    first = lane < HD_B

    def unit(blk, res):
        def rows(b_):
            return pl.ds(b_ * span + res, r, stride=dil) if dil > 1 else pl.ds(b_ * span, r)
        if blk == 0:
            k_prev, v_prev = kp_ref[rows(0), :], vp_ref[rows(0), :]
            valid = in_band & ((kcol >= r) | (n > 0))
        else:
            k_prev, v_prev = kc_ref[rows(blk - 1), :], vc_ref[rows(blk - 1), :]
            valid = in_band
        q = q_ref[rows(blk), :] * (HD_B ** -0.5)
        k2 = jnp.concatenate([k_prev, kc_ref[rows(blk), :]], axis=0).astype(BF16)
        v2 = jnp.concatenate([v_prev, vc_ref[rows(blk), :]], axis=0).astype(BF16)
        outs, lses = [], []
        for hh in range(2):
            qh = jnp.where(first == (hh == 0), q, 0.0).astype(BF16)
            s = _dot_nt(qh, k2) - slope_ref[0, hh:hh + 1, 0:1] * distf
            s = jnp.where(valid, s, NEG)
            m = jnp.broadcast_to(jnp.max(s, axis=1, keepdims=True), (r, 128))
            p = jnp.exp(s - jnp.tile(m, (1, 2)))
            l = jnp.broadcast_to(jnp.sum(p, axis=1, keepdims=True), (r, 128))
            outs.append(_dot(p.astype(BF16), v2) / l)
            lses.append(m + jnp.log(l))
        o_ref[rows(blk), :] = jnp.where(first, outs[0], outs[1])
        lse_ref[rows(blk), :] = jnp.where(first, lses[0], lses[1])

    for blk in range(nblk):
        if dil == 1:
            unit(blk, 0)
        else:
            def body(res, carry, blk=blk):
                unit(blk, res)
                return carry
            lax.fori_loop(0, dil, body, 0, unroll=min(dil, DIL_UNROLL))


def dil_band_attention(proj, g, bsz, t):
    window, dil = DIL_GROUPS[g]
    assert window // dil == BAND
    span = BAND * dil
    nblk = max(DIL_CHUNK_TOKENS // span, 1)
    ch = nblk * span
    nch = t // ch
    npair = N_HEAD_B // 2
    slopes = np.zeros((npair, 8, 128), np.float32)
    for h in range(N_HEAD_B):
        slopes[h // 2, h % 2, :] = _alibi_slope(h)

    def cur(c):
        return pl.BlockSpec((ch, 128), lambda b, n, hp, c=c: (b * nch + n, c * npair + hp))

    def prev(c):
        return pl.BlockSpec((span, 128),
                            lambda b, n, hp, c=c: (jnp.maximum((b * nch + n) * nblk - 1, 0), c * npair + hp))

    out_blk = pl.BlockSpec((ch, 128), lambda b, n, hp: (b * nch + n, hp))
    return pl.pallas_call(
        functools.partial(_dil_band_kernel, dil=dil, nblk=nblk),
        grid=(bsz, nch, npair),
        in_specs=[cur(3 * g), prev(3 * g + 1), cur(3 * g + 1), prev(3 * g + 2), cur(3 * g + 2),
                  pl.BlockSpec((1, 8, 128), lambda b, n, hp: (hp, 0, 0))],
        out_specs=[out_blk, out_blk],
        out_shape=[jax.ShapeDtypeStruct((bsz * t, D_MODEL), F32)] * 2,
        compiler_params=_cparams(("parallel", "arbitrary", "arbitrary")),
        name="dil_band%d" % g,
    )(proj, proj, proj, proj, proj, jnp.asarray(slopes))


def _mlstm_kernel(qp_ref, kp_ref, v_ref, op_ref, g_ref, gb_ref, cwq_ref, cwk_ref, cbq_ref, cbk_ref,
                  nw_ref, csq_ref, csk_ref, c0_ref, n0_ref, m0_ref,
                  hn_ref, c_out_ref, n_out_ref, m_out_ref,
                  c_s, n_s, m_s, xq_s, xk_s, *, lc, n_valid, nc):
    h = pl.program_id(1)
    c = pl.program_id(2)

    @pl.when(c == 0)
    def _():
        c_s[...] = c0_ref[0, 0]
        n_s[...] = n0_ref[0, 0]
        m_s[...] = m0_ref[0, 0]
        xq_s[0:8, :] = csq_ref[0]
        xk_s[0:8, :] = csk_ref[0]

    def conv(x_ref, xs, cw_ref, cb_ref):
        xs[8:8 + lc, :] = x_ref[...]
        acc = cb_ref[...] + cw_ref[0:1, :] * xs[pl.ds(8 - (CONV_W - 1), lc), :]
        for w in range(1, CONV_W):
            acc = acc + cw_ref[w:w + 1, :] * xs[pl.ds(8 - (CONV_W - 1) + w, lc), :]
        xs[0:8, :] = xs[lc:lc + 8, :]
        return _silu(acc)

    q = conv(qp_ref, xq_s, cwq_ref, cbq_ref)
    k = conv(kp_ref, xk_s, cwk_ref, cbk_ref) * (DH_C ** -0.5)
    v = v_ref[...]

    g = g_ref[...] + gb_ref[...]
    lane = lax.broadcasted_iota(jnp.int32, g.shape, 1)
    li_col = jnp.sum(jnp.where(lane == h, g, 0.0), axis=1, keepdims=True)
    lf_col = _log_sigmoid(jnp.sum(jnp.where(lane == N_HEAD_C + h, g, 0.0), axis=1, keepdims=True))
    if n_valid < lc:
        rows = lax.broadcasted_iota(jnp.int32, (lc, 1), 0)
        li_col = jnp.where(rows < n_valid, li_col, NEG)
        lf_col = jnp.where(rows < n_valid, lf_col, 0.0)
    row = lax.broadcasted_iota(jnp.int32, (lc, lc), 0)
    col = lax.broadcasted_iota(jnp.int32, (lc, lc), 1)
    eye = row == col
    tril = col <= row
    to_row = lambda x_col: jnp.sum(jnp.where(eye, x_col, 0.0), axis=0, keepdims=True)
    lf_row = to_row(lf_col)
    li_row = to_row(li_col)
    b_col = jnp.sum(jnp.where(tril, lf_row, 0.0), axis=1, keepdims=True)
    b_row = to_row(b_col)
    m_prev = m_s[...]
    d = jnp.where(tril, b_col - b_row + li_row, NEG)
    m_t = jnp.maximum(b_col + m_prev, jnp.max(d, axis=1, keepdims=True))
    dexp = jnp.where(tril, jnp.exp(d - m_t), 0.0)
    inter = jnp.exp(b_col + m_prev - m_t)
    qb, kb, vb = q.astype(BF16), k.astype(BF16), v.astype(BF16)
    w = _dot_nt(qb, kb) * dexp
    num = _dot(w.astype(BF16), vb) + inter * _dot(qb, c_s[...].astype(BF16))
    den = jnp.sum(w, axis=1, keepdims=True) + inter * jnp.sum(q * n_s[...], axis=1, keepdims=True)
    hh = num / jnp.maximum(jnp.abs(den), jnp.exp(-m_t))
    m_new = m_t[lc - 1:lc, :]
    b_last = b_col[lc - 1:lc, :]
    g_col = jnp.exp(b_last - b_col + li_col - m_new)
    decay = jnp.exp(b_last + m_prev - m_new)
    kg = k * g_col
    c_s[...] = decay * c_s[...] + _dot_tn(kg.astype(BF16), vb)
    n_s[...] = decay * n_s[...] + jnp.sum(kg, axis=0, keepdims=True)
    m_s[...] = m_new

    ho = hh * _sigmoid(op_ref[...])
    mu = jnp.mean(ho, axis=1, keepdims=True)
    var = jnp.mean(jnp.square(ho - mu), axis=1, keepdims=True)
    hn_ref[...] = (ho - mu) * lax.rsqrt(var + EPS) * nw_ref[...]

    @pl.when(c == nc - 1)
    def _():
        c_out_ref[0, 0] = c_s[...]
        n_out_ref[0, 0] = n_s[...]
        m_out_ref[0, 0] = m_s[...]


def mlstm_mix(proj, gates, b_i, b_f, conv_w, conv_b, norm_w, conv_state, c0, n0, m0, bsz, t, lc, n_valid):
    n = bsz * t
    nc = t // lc
    assert n_valid == lc or nc == 1
    nh = N_HEAD_C
    gb = jnp.zeros((1, 128), F32).at[0, :nh].set(b_i).at[0, nh:2 * nh].set(b_f)
    cs = jnp.pad(conv_state, ((0, 0), (8 - (CONV_W - 1), 0), (0, 0)))
    blk = lambda c0_: pl.BlockSpec((lc, DH_C), lambda b, h, c, c0_=c0_: (b * nc + c, c0_ + h))
    per_h = lambda rows, c0_: pl.BlockSpec((rows, DH_C), lambda b, h, c, c0_=c0_: (0, c0_ + h))
    st4 = lambda shape: pl.BlockSpec(shape, lambda b, h, c: (b, h, 0, 0))
    hn, c_out, n_out, m_out = pl.pallas_call(
        functools.partial(_mlstm_kernel, lc=lc, n_valid=n_valid, nc=nc),
        grid=(bsz, nh, nc),
        in_specs=[
            blk(0), blk(nh), blk(2 * nh), blk(3 * nh),
            pl.BlockSpec((lc, 128), lambda b, h, c: (b * nc + c, 0)),
            pl.BlockSpec((1, 128), lambda b, h, c: (0, 0)),
            per_h(CONV_W, 0), per_h(CONV_W, nh), per_h(1, 0), per_h(1, nh),
            per_h(1, 0),
            pl.BlockSpec((1, 8, DH_C), lambda b, h, c: (b, 0, h)),
            pl.BlockSpec((1, 8, DH_C), lambda b, h, c: (b, 0, nh + h)),
            st4((1, 1, DH_C, DH_C)), st4((1, 1, 1, DH_C)), st4((1, 1, 1, 1)),
        ],
        out_specs=[
            pl.BlockSpec((lc, DH_C), lambda b, h, c: (b * nc + c, h)),
            st4((1, 1, DH_C, DH_C)), st4((1, 1, 1, DH_C)), st4((1, 1, 1, 1)),
        ],
        out_shape=[
            jax.ShapeDtypeStruct((n, E_C), F32),
            jax.ShapeDtypeStruct((bsz, nh, DH_C, DH_C), F32),
            jax.ShapeDtypeStruct((bsz, nh, 1, DH_C), F32),
            jax.ShapeDtypeStruct((bsz, nh, 1, 1), F32),
        ],
        scratch_shapes=[pltpu.VMEM((DH_C, DH_C), F32), pltpu.VMEM((1, DH_C), F32), pltpu.VMEM((1, 1), F32),
                        pltpu.VMEM((lc + 8, DH_C), F32), pltpu.VMEM((lc + 8, DH_C), F32)],
        compiler_params=_cparams(("parallel", "parallel", "arbitrary")),
        name="mlstm",
    )(proj, proj, proj, proj, gates, gb, conv_w, conv_w, conv_b.reshape(1, -1), conv_b.reshape(1, -1),
      norm_w.reshape(1, -1), cs, cs, c0, n0.reshape(bsz, nh, 1, DH_C), m0.reshape(bsz, nh, 1, 1))
    return hn, c_out, n_out.reshape(bsz, nh, DH_C), m_out.reshape(bsz, nh)


TM_PROMPT = 512
TM_PROJ = 1024
TN_PROJ = 2048
FOX_TQ = 1024
FOX_PREP_TB = 512
MLSTM_LC = 512
SAMPLE_LC = 16


def fox_layer(xp, xs, nw, w_in, b_f, w_out, cache_kt, cache_vt, cache_lft, layer, page_table, bp, tp, bs, fw,
              final):
    e = N_HEAD_A * HD_A
    n_s = bs * N_NEW
    q_p, kt_p, vt_p, z_p, gates_p = fox_proj(xp, nw, w_in, bp, tp, TM_PROMPT)
    q_s, kt_s, vt_s, z_s, gates_s = fox_proj(xs, nw, w_in, 1, n_s, n_s)
    qb, qa, kbt, kat, vbt, lft_p = fox_prep(q_p, kt_p, vt_p, gates_p, b_f, bp, tp, FOX_PREP_TB)
    o_p = fox_attention_prompt(qb, qa, kbt, kat, vbt, bp, tp, FOX_TQ)
    yp = out_proj(o_p, 0, z_p, 0, w_out, xp, fw, TM_PROMPT, final)
    qbd = _block_diag_queries(q_s.reshape(bs, N_NEW, e), N_HEAD_A, HD_A)
    gnt = _new_token_major(gates_s[:, :N_HEAD_A].T, bs)
    o_s, lfn = fox_attention_sample(qbd, _new_token_major(kt_s[0], bs), _new_token_major(vt_s[0], bs), gnt, b_f,
                                    cache_kt, cache_vt, cache_lft, layer, page_table)
    ys = out_proj(o_s.reshape(n_s, e), 0, z_s, 0, w_out, xs, fw, n_s, final)
    heads_t = lambda a_t, b_, t_: a_t.reshape(b_, N_HEAD_A, HD_A, t_).transpose(0, 3, 1, 2)
    new_p = (heads_t(kt_p, bp, tp), heads_t(vt_p, bp, tp), lft_p.transpose(0, 2, 1))
    new_s = (kt_s[0].T.reshape(bs, N_NEW, N_HEAD_A, HD_A), vt_s[0].T.reshape(bs, N_NEW, N_HEAD_A, HD_A),
             lfn[:, :, :N_NEW].transpose(0, 2, 1))
    return yp, ys, new_p, new_s


def dil_layer(xp, xs, nw, w_in, w_out, caches, bp, tp, bs):
    e = N_HEAD_B * HD_B
    no_gate = jnp.zeros((D_MODEL, 1), F32)
    proj_p, _ = norm_proj(xp, nw, w_in, no_gate, TM_PROJ, TN_PROJ)
    proj_s, _ = norm_proj(xs, nw, w_in, no_gate, xs.shape[0], TN_PROJ)
    z_col = 3 * len(DIL_GROUPS)
    res_p = [dil_band_attention(proj_p, g, bp, tp) for g in range(len(DIL_GROUPS))]
    yp = dil_out_proj([r[0] for r in res_p], [r[1] for r in res_p], proj_p, z_col, w_out, xp, TM_PROMPT)
    ps3 = proj_s.reshape(bs, N_NEW, -1)
    res_s, new_p, new_s = [], [], []
    for g, (window, _) in enumerate(DIL_GROUPS):
        part = lambda c: ps3[:, :, (3 * g + c) * e:(3 * g + c + 1) * e]
        buf_t = caches[g].transpose(0, 2, 3, 4, 1).reshape(bs, 2, e, window)
        knt = jnp.pad(part(1).transpose(0, 2, 1), ((0, 0), (0, 0), (0, 128 - N_NEW)))
        vnt = jnp.pad(part(2).transpose(0, 2, 1), ((0, 0), (0, 0), (0, 128 - N_NEW)))
        o_s, lse_s, new_buf_t = dil_sample_attention(_block_diag_queries(part(0), N_HEAD_B, HD_B), knt, vnt, buf_t, g)
        res_s.append((o_s, lse_s))
        keep = min(window, tp)
        rows = proj_p.reshape(bp, tp, -1)[:, tp - keep:, (3 * g + 1) * e:(3 * g + 3) * e]
        new_p.append(rows.reshape(bp, keep, 2, N_HEAD_B, HD_B))
        new_s.append(new_buf_t.reshape(bs, 2, N_HEAD_B, HD_B, window).transpose(0, 4, 1, 2, 3))
    flat = lambda a: a.reshape(bs * N_NEW, e)
    ys = dil_out_proj([flat(r[0]) for r in res_s], [flat(r[1]) for r in res_s], proj_s, z_col, w_out, xs,
                      xs.shape[0])
    return yp, ys, tuple(new_p), tuple(new_s)


def mlstm_layer(xp, xs, nw, w_in, b_i, b_f, conv_w, conv_b, norm_w, w_out, conv_state, c0, n0, m0, bp, tp, bs, fw):
    n_main = 5 * E_C
    w_main, w_gate = w_in[:, :n_main], w_in[:, n_main:]
    proj_p, gates_p = norm_proj(xp, nw, w_main, w_gate, TM_PROJ, TN_PROJ)
    proj_s, gates_s = norm_proj(xs, nw, w_main, w_gate, xs.shape[0], TN_PROJ)
    zeros = lambda *shape: jnp.zeros(shape, F32)
    hn_p, c_p, n_p, m_p = mlstm_mix(proj_p, gates_p, b_i, b_f, conv_w, conv_b, norm_w,
                                    zeros(bp, CONV_W - 1, 2 * E_C), zeros(bp, N_HEAD_C, DH_C, DH_C),
                                    zeros(bp, N_HEAD_C, DH_C), zeros(bp, N_HEAD_C), bp, tp, MLSTM_LC, MLSTM_LC)
    yp = out_proj(hn_p, 0, proj_p, 4, w_out, xp, fw, TM_PROMPT, False)
    pad_t = lambda a: jnp.pad(a.reshape(bs, N_NEW, -1), ((0, 0), (0, SAMPLE_LC - N_NEW), (0, 0))).reshape(
        bs * SAMPLE_LC, -1)
    hn_s, c_s, n_s, m_s = mlstm_mix(pad_t(proj_s), pad_t(gates_s), b_i, b_f, conv_w, conv_b, norm_w,
                                    conv_state, c0, n0, m0, bs, SAMPLE_LC, SAMPLE_LC, N_NEW)
    hn_s = hn_s.reshape(bs, SAMPLE_LC, E_C)[:, :N_NEW].reshape(bs * N_NEW, E_C)
    ys = out_proj(hn_s, 0, proj_s, 4, w_out, xs, fw, xs.shape[0], False)
    keep_p = min(tp, CONV_W - 1)
    qk_p = proj_p.reshape(bp, tp, -1)[:, tp - keep_p:, :2 * E_C]
    conv_p = jnp.concatenate([zeros(bp, CONV_W - 1 - keep_p, 2 * E_C), qk_p], axis=1)
    qk_s = proj_s[:, :2 * E_C].reshape(bs, N_NEW, 2 * E_C)
    conv_s = jnp.concatenate([conv_state, qk_s], axis=1)[:, N_NEW:]
    return yp, ys, (c_p, n_p, m_p, conv_p), (c_s, n_s, m_s, conv_s)


def kernel(x_prompt, x_sample, cache_fox_k, cache_fox_v, cache_fox_logf, cache_dil0_kv, cache_dil1_kv, cache_dil2_kv, state_mlstm_C, state_mlstm_n, state_mlstm_m, state_mlstm_conv, page_table, norm_w, final_norm_w, fox_w_in, fox_b_f, fox_w_out, dil_w_in, dil_w_out, mlstm_w_in, mlstm_b_i, mlstm_b_f, mlstm_conv_w, mlstm_conv_b, mlstm_norm_w, mlstm_w_out):
    bp, tp, d = x_prompt.shape
    bs, ts, _ = x_sample.shape
    assert ts == N_NEW and d == D_MODEL
    depth = norm_w.shape[0]
    dil_caches = (cache_dil0_kv, cache_dil1_kv, cache_dil2_kv)
    n_la, n_pool = cache_fox_k.shape[:2]
    fox_kt = cache_fox_k.transpose(0, 1, 3, 4, 2).reshape(n_la, n_pool, N_HEAD_A * HD_A, PAGE)
    fox_vt = cache_fox_v.transpose(0, 1, 3, 4, 2).reshape(n_la, n_pool, N_HEAD_A * HD_A, PAGE)
    fox_lft = cache_fox_logf.transpose(0, 1, 3, 2)
    xp = x_prompt.reshape(bp * tp, d)
    xs = x_sample.reshape(bs * ts, d)
    fox_p, fox_s, dil_p, dil_s, ml_p, ml_s = [], [], [], [], [], []
    for i in range(depth):
        j, kind = divmod(i, 3)
        final = i == depth - 1
        if kind == 0:
            xp, xs, new_p, new_s = fox_layer(xp, xs, norm_w[i], fox_w_in[j], fox_b_f[j], fox_w_out[j],
                                             fox_kt, fox_vt, fox_lft, j, page_table,
                                             bp, tp, bs, final_norm_w, final)
            fox_p.append(new_p)
            fox_s.append(new_s)
        elif kind == 1:
            assert not final
            xp, xs, new_p, new_s = dil_layer(xp, xs, norm_w[i], dil_w_in[j], dil_w_out[j],
                                             tuple(c[j] for c in dil_caches), bp, tp, bs)
            dil_p.append(new_p)
            dil_s.append(new_s)
        else:
            assert not final
            xp, xs, new_p, new_s = mlstm_layer(xp, xs, norm_w[i], mlstm_w_in[j], mlstm_b_i[j], mlstm_b_f[j],
                                               mlstm_conv_w[j], mlstm_conv_b[j], mlstm_norm_w[j], mlstm_w_out[j],
                                               state_mlstm_conv[j], state_mlstm_C[j], state_mlstm_n[j],
                                               state_mlstm_m[j], bp, tp, bs, final_norm_w)
            ml_p.append(new_p)
            ml_s.append(new_s)
    stk = lambda lst, f: jnp.stack([e[f] for e in lst], axis=0)
    return (xp.reshape(bp, tp, d), xs.reshape(bs, ts, d),
            stk(fox_p, 0), stk(fox_p, 1), stk(fox_p, 2),
            stk(fox_s, 0), stk(fox_s, 1), stk(fox_s, 2),
            stk(dil_p, 0), stk(dil_p, 1), stk(dil_p, 2),
            stk(dil_s, 0), stk(dil_s, 1), stk(dil_s, 2),
            stk(ml_p, 0), stk(ml_p, 1), stk(ml_p, 2), stk(ml_p, 3),
            stk(ml_s, 0), stk(ml_s, 1), stk(ml_s, 2), stk(ml_s, 3))
```
